```python
import jax, jax.numpy as jnp
from jax import lax
import numpy as np

D_MODEL = 2048
BATCH = 2
SEQ = 4096
DEPTH = 1
DEC_BATCH = 32
DEC_SEQ = 8
PAST_LEN = 16384
PAGE_SIZE = 128

H_R = 8
DK_R = 128
DV_R = 256
H_S = 16
HD_S = 64
W_QK_R = H_R * DK_R
W_V_R = H_R * DV_R
W_S = H_S * HD_S
COLS = [W_QK_R, W_QK_R, W_V_R, W_V_R, W_S, W_S, W_S, W_S]
SPLITS = [int(s) for s in np.cumsum(COLS)[:-1]]
D_IN = int(sum(COLS))
CHUNK = 128
Q_BLOCK = 128
ROPE_BASE = 10000.0
EPS = 1e-6
SB_BIAS_HI = -1.0
SB_BIAS_LO = -10.0

kernel_name = "retention_stickbreaking_gated_hybrid_step"


def rms_norm(x, gain):
    xf = x.astype(jnp.float32)
    inv = lax.rsqrt(jnp.mean(xf * xf, axis=-1, keepdims=True) + EPS)
    return (xf * inv).astype(x.dtype) * gain


def rotary(x, pos):
    half = x.shape[-1] // 2
    inv_freq = ROPE_BASE ** (-jnp.arange(half, dtype=jnp.float32) / half)
    ang = pos.astype(jnp.float32)[:, None] * inv_freq[None, :]
    cos = jnp.cos(ang)[:, None, :].astype(x.dtype)
    sin = jnp.sin(ang)[:, None, :].astype(x.dtype)
    x1, x2 = x[..., :half], x[..., half:]
    return jnp.concatenate([x1 * cos - x2 * sin, x1 * sin + x2 * cos], axis=-1)


def retention_log_gamma():
    return jnp.log(1.0 - 2.0 ** (-5.0 - jnp.arange(H_R, dtype=jnp.float32)))


def retention_chunk(state, q, k, v, log_gamma):
    C = q.shape[1]
    idx = jnp.arange(C, dtype=jnp.float32)
    diff = idx[:, None] - idx[None, :]
    causal = diff >= 0
    decay = jnp.where(causal[None], jnp.exp(log_gamma[:, None, None] * jnp.maximum(diff, 0.0)[None]), 0.0)
    scores = jnp.einsum('bihd,bjhd->bhij', q, k) * decay[None].astype(q.dtype)
    inner = jnp.einsum('bhij,bjhe->bihe', scores, v)
    q_dec = jnp.exp(log_gamma[None, :] * (idx[:, None] + 1.0)).astype(q.dtype)
    cross = jnp.einsum('bihd,bhde->bihe', q, state) * q_dec[None, :, :, None]
    k_dec = jnp.exp(log_gamma[None, :] * (C - 1.0 - idx[:, None])).astype(k.dtype)
    chunk_decay = jnp.exp(log_gamma * C).astype(state.dtype)
    new_state = chunk_decay[None, :, None, None] * state + jnp.einsum('bjhd,bjhe->bhde', k * k_dec[None, :, :, None], v)
    return new_state, inner + cross


def retention_prompt(q, k, v, log_gamma):
    B, S, H, DK = q.shape
    DV = v.shape[-1]
    nc = S // CHUNK

    def to_chunks(t):
        return t.reshape(B, nc, CHUNK, H, t.shape[-1]).transpose(1, 0, 2, 3, 4)

    state0 = jnp.zeros((B, H, DK, DV), v.dtype)

    def step(state, xs):
        qc, kc, vc = xs
        return retention_chunk(state, qc, kc, vc, log_gamma)

    final_state, out = lax.scan(step, state0, (to_chunks(q), to_chunks(k), to_chunks(v)))
    return out.transpose(1, 0, 2, 3, 4).reshape(B, S, H, DV), final_state


def stick_breaking(q, k, v, mask, bias):
    z = jnp.einsum('bqhd,blhd->bhql', q, k, preferred_element_type=jnp.float32) * (HD_S ** -0.5)
    z = z + bias.astype(jnp.float32)[None, :, None, None]
    log_beta = jax.nn.log_sigmoid(z)
    log_1mb = jnp.where(mask, jax.nn.log_sigmoid(-z), 0.0)
    after = lax.cumsum(log_1mb, axis=3, reverse=True) - log_1mb
    A = jnp.where(mask, jnp.exp(log_beta + after), 0.0)
    return jnp.einsum('bhql,blhd->bqhd', A.astype(v.dtype), v)


def stick_breaking_prompt(q, k, v, bias):
    B, S, H, D = q.shape
    nb = S // Q_BLOCK
    qb = q.reshape(B, nb, Q_BLOCK, H, D).transpose(1, 0, 2, 3, 4)
    kpos = jnp.arange(S)

    def block(args):
        qblk, b = args
        qpos = b * Q_BLOCK + jnp.arange(Q_BLOCK)
        mask = kpos[None, :] < qpos[:, None]
        return stick_breaking(qblk, k, v, mask, bias)

    out = lax.map(block, (qb, jnp.arange(nb)))
    return out.transpose(1, 0, 2, 3, 4).reshape(B, S, H, D)


def stick_breaking_sample(q, k_new, v_new, cache_k, cache_v, page_table, bias):
    T = q.shape[1]
    past = page_table.shape[1] * PAGE_SIZE
    L = past + T
    mask = jnp.arange(L)[None, :] < (past + jnp.arange(T))[:, None]

    def one(args):
        q1, kn, vn, pages = args
        kp = cache_k[pages].reshape(past, H_S, HD_S)
        vp = cache_v[pages].reshape(past, H_S, HD_S)
        k_all = jnp.concatenate([kp, kn], axis=0)
        v_all = jnp.concatenate([vp, vn], axis=0)
        return stick_breaking(q1[None], k_all[None], v_all[None], mask, bias)[0]

    return lax.map(one, (q, k_new, v_new, page_table))


def mixer_inputs(x, c, pos, norm_gain, w_ada, b_ada, w_in, qn_gain, kn_gain):
    B, T, _ = x.shape
    mod = jax.nn.silu(c) @ w_ada + b_ada
    shift, scale, gate = jnp.split(mod, 3, axis=-1)
    h = rms_norm(x, norm_gain) * (1.0 + scale[:, None, :]) + shift[:, None, :]
    proj = h @ w_in
    q_r, k_r, v_r, g_r, q_s, k_s, v_s, g_s = jnp.split(proj, SPLITS, axis=-1)
    q_r = rotary(q_r.reshape(B, T, H_R, DK_R), pos)
    k_r = rotary(k_r.reshape(B, T, H_R, DK_R), pos) * (DK_R ** -0.5)
    v_r = v_r.reshape(B, T, H_R, DV_R)
    q_s = rms_norm(q_s.reshape(B, T, H_S, HD_S), qn_gain)
    k_s = rms_norm(k_s.reshape(B, T, H_S, HD_S), kn_gain)
    v_s = v_s.reshape(B, T, H_S, HD_S)
    return h, gate, g_r, g_s, q_r, k_r, v_r, q_s, k_s, v_s


def merge_output(x, h, gate, o_r, g_r, o_s, g_s, gn_r, w_pr, w_ps, w_mg, b_mg, w_out):
    B, T, _ = x.shape
    br_r = rms_norm(o_r, gn_r).reshape(B, T, W_V_R) * jax.nn.silu(g_r)
    br_s = o_s.reshape(B, T, W_S) * jax.nn.silu(g_s)
    y_r = br_r @ w_pr
    y_s = br_s @ w_ps
    gates = jax.nn.sigmoid(h @ w_mg + b_mg)
    gate_r, gate_s = jnp.split(gates, 2, axis=-1)
    merged = gate_r * y_r + gate_s * y_s
    return x + gate[:, None, :] * (merged @ w_out)


def setup_inputs(seed: int = 0) -> dict:
    key = jax.random.key(seed)
    ks = jax.random.split(key, 24)
    n_pages = PAST_LEN // PAGE_SIZE
    n_used = DEC_BATCH * n_pages
    n_pool = n_used + n_used // 4
    nrm = jax.random.normal
    f32 = jnp.float32
    page_table = jax.random.permutation(ks[0], n_pool)[:n_used].reshape(DEC_BATCH, n_pages).astype(jnp.int32)
    return {
        "x_prompt": nrm(ks[1], (BATCH, SEQ, D_MODEL), f32),
        "x_sample": nrm(ks[2], (DEC_BATCH, DEC_SEQ, D_MODEL), f32),
        "cache_sb_k": nrm(ks[3], (n_pool, PAGE_SIZE, H_S, HD_S), f32),
        "cache_sb_v": nrm(ks[4], (n_pool, PAGE_SIZE, H_S, HD_S), f32),
        "state_ret": 0.5 * nrm(ks[5], (DEC_BATCH, H_R, DK_R, DV_R), f32),
        "page_table": page_table,
        "c_prompt": nrm(ks[6], (BATCH, D_MODEL), f32),
        "c_sample": nrm(ks[7], (DEC_BATCH, D_MODEL), f32),
        "norm_gain": 1.0 + 0.05 * nrm(ks[8], (D_MODEL,), f32),
        "w_ada": nrm(ks[9], (D_MODEL, 3 * D_MODEL), f32) * D_MODEL ** -0.5,
        "b_ada": 0.01 * nrm(ks[10], (3 * D_MODEL,), f32),
        "w_in": nrm(ks[11], (D_MODEL, D_IN), f32) * D_MODEL ** -0.5,
        "qn_gain": 1.0 + 0.05 * nrm(ks[12], (H_S, HD_S), f32),
        "kn_gain": 1.0 + 0.05 * nrm(ks[13], (H_S, HD_S), f32),
        "sb_bias": jnp.linspace(SB_BIAS_HI, SB_BIAS_LO, H_S, dtype=f32) + 0.1 * nrm(ks[20], (H_S,), f32),
        "gn_r": 1.0 + 0.05 * nrm(ks[14], (H_R, DV_R), f32),
        "w_pr": nrm(ks[15], (W_V_R, D_MODEL), f32) * W_V_R ** -0.5,
        "w_ps": nrm(ks[16], (W_S, D_MODEL), f32) * W_S ** -0.5,
        "w_mg": nrm(ks[17], (D_MODEL, 2 * D_MODEL), f32) * D_MODEL ** -0.5,
        "b_mg": 0.01 * nrm(ks[18], (2 * D_MODEL,), f32),
        "w_out": nrm(ks[19], (D_MODEL, D_MODEL), f32) * D_MODEL ** -0.5,
    }


def reference(x_prompt, x_sample, cache_sb_k, cache_sb_v, state_ret, page_table, c_prompt, c_sample,
              norm_gain, w_ada, b_ada, w_in, qn_gain, kn_gain, sb_bias, gn_r, w_pr, w_ps, w_mg, b_mg, w_out):
    log_gamma = retention_log_gamma()
    seq = x_prompt.shape[1]
    dec_seq = x_sample.shape[1]
    past = page_table.shape[1] * PAGE_SIZE
    pos_prompt = jnp.arange(seq)
    pos_sample = past + jnp.arange(dec_seq)
    y_p, y_s = x_prompt, x_sample
    for _ in range(DEPTH):
        h, gate, g_r, g_s, q_r, k_r, v_r, q_s, k_s, v_s = mixer_inputs(
            y_p, c_prompt, pos_prompt, norm_gain, w_ada, b_ada, w_in, qn_gain, kn_gain)
        o_r, ret_state_prompt = retention_prompt(q_r, k_r, v_r, log_gamma)
        o_s = stick_breaking_prompt(q_s, k_s, v_s, sb_bias)
        sb_k_prompt, sb_v_prompt = k_s, v_s
        y_p = merge_output(y_p, h, gate, o_r, g_r, o_s, g_s, gn_r, w_pr, w_ps, w_mg, b_mg, w_out)

        h2, gate2, g_r2, g_s2, q_r2, k_r2, v_r2, q_s2, k_s2, v_s2 = mixer_inputs(
            y_s, c_sample, pos_sample, norm_gain, w_ada, b_ada, w_in, qn_gain, kn_gain)
        ret_state_sample, o_r2 = retention_chunk(state_ret, q_r2, k_r2, v_r2, log_gamma)
        o_s2 = stick_breaking_sample(q_s2, k_s2, v_s2, cache_sb_k, cache_sb_v, page_table, sb_bias)
        sb_k_sample, sb_v_sample = k_s2, v_s2
        y_s = merge_output(y_s, h2, gate2, o_r2, g_r2, o_s2, g_s2, gn_r, w_pr, w_ps, w_mg, b_mg, w_out)
    return (y_p, y_s, ret_state_prompt, ret_state_sample, sb_k_prompt, sb_v_prompt, sb_k_sample, sb_v_sample)
```

```python
import functools
import math

import jax
import jax.numpy as jnp
from jax import lax
from jax.experimental import pallas as pl
from jax.experimental.pallas import tpu as pltpu

F32 = jnp.float32
BF16 = jnp.bfloat16

H_R, DK_R, DV_R = 8, 128, 256
H_S, HD_S = 16, 64
W_QK_R = H_R * DK_R
W_V_R = H_R * DV_R
W_S = H_S * HD_S
CHUNK = 128
ROPE_BASE = 10000.0
EPS = 1e-6

LANES = 128
MXU_DIM = 256
VMEM_LIMIT = 48 * 1024 * 1024

SB_BLOCK = 256
PAGES_PER_STEP = 8

_NT = (((1,), (1,)), ((), ()))
_TN = (((0,), (0,)), ((), ()))


def _cparams(sem):
    return pltpu.CompilerParams(dimension_semantics=sem, vmem_limit_bytes=VMEM_LIMIT)


def _dot(a, b):
    return jnp.dot(a, b, preferred_element_type=F32)


def _split_bf16(x):
    hi = x.astype(BF16)
    lo = (x - hi.astype(F32)).astype(BF16)
    return hi, lo


def _ada_kernel(c_ref, w_ref, b_ref, o_ref):
    c = c_ref[...]
    a = c * jax.nn.sigmoid(c)
    o_ref[...] = jnp.dot(a, w_ref[...], preferred_element_type=F32,
                         precision=lax.Precision.HIGHEST) + b_ref[...]


def _ada(c_all, w_ada, b_ada):
    r, d = c_all.shape
    n = w_ada.shape[1]
    tn = 512
    return pl.pallas_call(
        _ada_kernel,
        grid=(n // tn,),
        in_specs=[pl.BlockSpec((r, d), lambda j: (0, 0)),
                  pl.BlockSpec((d, tn), lambda j: (0, j)),
                  pl.BlockSpec((1, tn), lambda j: (0, j))],
        out_specs=pl.BlockSpec((r, tn), lambda j: (0, j)),
        out_shape=jax.ShapeDtypeStruct((r, n), F32),
        compiler_params=_cparams(("arbitrary",)),
        name="ada",
    )(c_all, w_ada, b_ada.reshape(1, n))


def _prep_kernel(x_ref, g_ref, sh_ref, sc_ref, h_ref):
    x = x_ref[0]
    inv = lax.rsqrt(jnp.mean(x * x, axis=-1, keepdims=True) + EPS)
    h = (x * inv) * g_ref[...] * (1.0 + sc_ref[0]) + sh_ref[0]
    h_ref[0] = h.astype(BF16)


def _mod_spec(mod3, part, tm, tn, d, ngrid):
    per_row = mod3.shape[1] != 1
    nb = d // tn
    if ngrid == 2:
        if per_row:
            return pl.BlockSpec((1, tm, tn), lambda b, i: (b, i, part * nb))
        return pl.BlockSpec((1, 1, tn), lambda b, i: (b, 0, part * nb))
    if per_row:
        return pl.BlockSpec((1, tm, tn), lambda b, i, j: (b, i, part * nb + j))
    return pl.BlockSpec((1, 1, tn), lambda b, i, j: (b, 0, part * nb + j))


def _prep(x3, mod3, norm_gain, tm):
    bx, tx, d = x3.shape
    return pl.pallas_call(
        _prep_kernel,
        grid=(bx, tx // tm),
        in_specs=[pl.BlockSpec((1, tm, d), lambda b, i: (b, i, 0)),
                  pl.BlockSpec((1, d), lambda b, i: (0, 0)),
                  _mod_spec(mod3, 0, tm, d, d, 2),
                  _mod_spec(mod3, 1, tm, d, d, 2)],
        out_specs=pl.BlockSpec((1, tm, d), lambda b, i: (b, i, 0)),
        out_shape=jax.ShapeDtypeStruct((bx, tx, d), BF16),
        compiler_params=_cparams(("parallel", "parallel")),
        name="prep",
    )(x3, norm_gain.reshape(1, d), mod3, mod3)


def _mm_kernel(a_ref, w_ref, *rest, epi):
    acc = _dot(a_ref[0], w_ref[...])
    epi(acc, *rest)


def _mm(a3, w, col0, n, epi, tm, tn, extras, extra_specs, out_dtypes, name):
    bx, tx, k = a3.shape
    cb0 = col0 // tn
    out_shape = [jax.ShapeDtypeStruct((bx, tx, n), dt) for dt in out_dtypes]
    out_specs = [pl.BlockSpec((1, tm, tn), lambda b, i, j: (b, i, j)) for _ in out_dtypes]
    return pl.pallas_call(
        functools.partial(_mm_kernel, epi=epi),
        grid=(bx, tx // tm, n // tn),
        in_specs=[pl.BlockSpec((1, tm, k), lambda b, i, j: (b, i, 0)),
                  pl.BlockSpec((k, tn), lambda b, i, j: (0, cb0 + j))] + list(extra_specs),
        out_specs=out_specs,
        out_shape=out_shape,
        compiler_params=_cparams(("parallel", "parallel", "arbitrary")),
        name=name,
    )(a3, w, *extras)


def _epi_rot(acc, cos_ref, sin_ref, o_ref, *, tn):
    j = pl.program_id(2)
    scale = jnp.where(j * tn >= W_QK_R, DK_R ** -0.5, 1.0).astype(F32)
    cos = cos_ref[...]
    sin = sin_ref[...]
    for hh in range(tn // DK_R):
        xh = acc[:, hh * DK_R:(hh + 1) * DK_R]
        r = pltpu.roll(xh, DK_R // 2, axis=1)
        o_ref[0, :, hh * DK_R:(hh + 1) * DK_R] = ((xh * cos + r * sin) * scale).astype(o_ref.dtype)


def _epi_plain(acc, o_ref):
    o_ref[0] = acc.astype(o_ref.dtype)


def _epi_silu(acc, o_ref):
    o_ref[0] = (acc * jax.nn.sigmoid(acc)).astype(o_ref.dtype)


def _head_norm(acc, g_ref, gain_ref, tn):
    g = g_ref[...]
    parts = []
    for c in range(tn // MXU_DIM):
        a = acc[:, c * MXU_DIM:(c + 1) * MXU_DIM]
        hi, lo = _split_bf16(a * a)
        ms = _dot(hi, g) + _dot(lo, g)
        parts.append(a * lax.rsqrt(ms + EPS))
    y = parts[0] if len(parts) == 1 else jnp.concatenate(parts, axis=1)
    return y * gain_ref[...]


def _epi_qs(acc, g_ref, gain_ref, o_ref, *, tn):
    o_ref[0] = (_head_norm(acc, g_ref, gain_ref, tn) * HD_S ** -0.5).astype(o_ref.dtype)


def _epi_ks1(acc, g_ref, gain_ref, o_ref, *, tn):
    o_ref[0] = _head_norm(acc, g_ref, gain_ref, tn)


def _epi_gates(acc, b_ref, o_ref):
    o_ref[0] = jax.nn.sigmoid(acc + b_ref[...]).astype(o_ref.dtype)


def _mmt_kernel(a_ref, wt_ref, *rest, norm, tn, blk):
    if norm:
        g_ref, gain_ref, o_ref, ob_ref = rest
    else:
        o_ref, ob_ref = rest
    acc = lax.dot_general(wt_ref[...], a_ref[0], _NT, preferred_element_type=F32)
    if norm:
        g = g_ref[...]
        parts = []
        for c in range(tn // MXU_DIM):
            a = acc[c * MXU_DIM:(c + 1) * MXU_DIM, :]
            hi, lo = _split_bf16(a * a)
            ms = _dot(g, hi) + _dot(g, lo)
            parts.append(a * lax.rsqrt(ms + EPS))
        acc = (parts[0] if len(parts) == 1 else jnp.concatenate(parts, axis=0)) * gain_ref[...]
    o_ref[0] = acc
    for kb in range(ob_ref.shape[1]):
        ob_ref[0, kb] = acc[:, kb * blk:(kb + 1) * blk].astype(ob_ref.dtype)


def _mm_t(a3, wt, norm_inputs, tm, tn, blk, name):
    bx, tx, k = a3.shape
    n = wt.shape[0]
    norm = norm_inputs is not None
    extra, extra_specs = (), []
    if norm:
        extra = norm_inputs
        extra_specs = [pl.BlockSpec((MXU_DIM, MXU_DIM), lambda b, i, j: (0, 0)),
                       pl.BlockSpec((tn, 1), lambda b, i, j: (j, 0))]
    return pl.pallas_call(
        functools.partial(_mmt_kernel, norm=norm, tn=tn, blk=blk),
        grid=(bx, tx // tm, n // tn),
        in_specs=[pl.BlockSpec((1, tm, k), lambda b, i, j: (b, i, 0)),
                  pl.BlockSpec((tn, k), lambda b, i, j: (j, 0))] + extra_specs,
        out_specs=[pl.BlockSpec((1, tn, tm), lambda b, i, j: (b, j, i)),
                   pl.BlockSpec((1, tm // blk, tn, blk), lambda b, i, j: (b, i, j, 0))],
        out_shape=[jax.ShapeDtypeStruct((bx, n, tx), F32),
                   jax.ShapeDtypeStruct((bx, tx // blk, n, blk), BF16)],
        compiler_params=_cparams(("parallel", "parallel", "arbitrary")),
        name=name,
    )(a3, wt, *extra)


def _mixer_group(x3, mod3, cos_t, sin_t, norm_gain, w_in_b, w_mg_b, b_mg, qn_gain, kn_gain, tm, feature_major_kv):
    d = x3.shape[2]
    tn = 512
    h = _prep(x3, mod3, norm_gain, tm)
    avg = jnp.where((jnp.arange(MXU_DIM)[:, None] // HD_S) == (jnp.arange(MXU_DIM)[None, :] // HD_S),
                    1.0 / HD_S, 0.0).astype(BF16)
    avg_spec = pl.BlockSpec((MXU_DIM, MXU_DIM), lambda b, i, j: (0, 0))
    gain_spec = pl.BlockSpec((1, tn), lambda b, i, j: (0, j))
    tab_spec = pl.BlockSpec((tm, DK_R), lambda b, i, j: (i, 0))

    c0 = 0
    (qk_r,) = _mm(h, w_in_b, c0, 2 * W_QK_R, functools.partial(_epi_rot, tn=tn), tm, tn,
                  (cos_t, sin_t), (tab_spec, tab_spec), (BF16,), "proj_qk_r")
    c0 += 2 * W_QK_R
    (v_r,) = _mm(h, w_in_b, c0, W_V_R, _epi_plain, tm, tn, (), (), (BF16,), "proj_v_r")
    c0 += W_V_R
    (sg_r,) = _mm(h, w_in_b, c0, W_V_R, _epi_silu, tm, tn, (), (), (BF16,), "proj_g_r")
    c0 += W_V_R
    (q_s,) = _mm(h, w_in_b, c0, W_S, functools.partial(_epi_qs, tn=tn), tm, tn,
                 (avg, qn_gain.reshape(1, W_S)), (avg_spec, gain_spec), (BF16,), "proj_q_s")
    c0 += W_S
    if feature_major_kv:
        blk = min(SB_BLOCK, tm)
        k_s, k_sb = _mm_t(h, w_in_b[:, c0:c0 + W_S].T, (avg, kn_gain.reshape(W_S, 1)), tm, tn, blk, "proj_k_s_t")
        c0 += W_S
        v_s, v_sb = _mm_t(h, w_in_b[:, c0:c0 + W_S].T, None, tm, tn, blk, "proj_v_s_t")
        c0 += W_S
    else:
        (k_s,) = _mm(h, w_in_b, c0, W_S, functools.partial(_epi_ks1, tn=tn), tm, tn,
                     (avg, kn_gain.reshape(1, W_S)), (avg_spec, gain_spec), (F32,), "proj_k_s")
        c0 += W_S
        (v_s,) = _mm(h, w_in_b, c0, W_S, _epi_plain, tm, tn, (), (), (F32,), "proj_v_s")
        c0 += W_S
        k_sb = v_sb = None
    (sg_s,) = _mm(h, w_in_b, c0, W_S, _epi_silu, tm, tn, (), (), (BF16,), "proj_g_s")
    (gates,) = _mm(h, w_mg_b, 0, 2 * d, _epi_gates, tm, tn,
                   (b_mg.reshape(1, 2 * d),), (gain_spec,), (BF16,), "proj_gates")
    return dict(qk_r=qk_r, v_r=v_r, sg_r=sg_r, q_s=q_s, k_s=k_s, k_sb=k_sb, v_s=v_s, v_sb=v_sb,
                sg_s=sg_s, gates=gates)


def _ret_kernel(q_ref, k_ref, v_ref, sg_ref, gn_ref, s0_ref, o_ref, st_ref, *, rows, chunk):
    @pl.when(pl.program_id(1) == 0)
    def _():
        st_ref[...] = s0_ref[...]

    ri = lax.broadcasted_iota(jnp.int32, (rows, rows), 0)
    ci = lax.broadcasted_iota(jnp.int32, (rows, rows), 1)
    diff = (ri - ci).astype(F32)
    idx = lax.broadcasted_iota(jnp.int32, (rows, 1), 0).astype(F32)
    for h in range(H_R):
        lg = math.log(1.0 - 2.0 ** (-5.0 - h))
        decay = jnp.where(diff >= 0.0, jnp.exp(lg * jnp.maximum(diff, 0.0)), 0.0)
        q_dec = jnp.exp(lg * (idx + 1.0))
        k_dec = jnp.exp(lg * (chunk - 1.0 - idx))
        q = q_ref[0, :, h * DK_R:(h + 1) * DK_R]
        k = k_ref[0, :, h * DK_R:(h + 1) * DK_R]
        v = v_ref[0, :, h * DV_R:(h + 1) * DV_R]
        st = st_ref[0, h]
        scores = lax.dot_general(q, k, _NT, preferred_element_type=F32) * decay
        inner = _dot(scores.astype(BF16), v)
        cross = _dot(q, st.astype(BF16)) * q_dec
        kd = (k.astype(F32) * k_dec).astype(BF16)
        st_ref[0, h] = math.exp(lg * chunk) * st + lax.dot_general(kd, v, _TN, preferred_element_type=F32)
        o = inner + cross
        inv = lax.rsqrt(jnp.mean(o * o, axis=-1, keepdims=True) + EPS)
        sl = slice(h * DV_R, (h + 1) * DV_R)
        o_ref[0, :, sl] = ((o * inv) * gn_ref[:, sl] * sg_ref[0, :, sl].astype(F32)).astype(o_ref.dtype)


def _retention(qk_r, v_r, sg_r, gn_r, state0, rows, chunk):
    bx, tx, _ = v_r.shape
    nc = tx // rows
    return pl.pallas_call(
        functools.partial(_ret_kernel, rows=rows, chunk=chunk),
        grid=(bx, nc),
        in_specs=[pl.BlockSpec((1, rows, W_QK_R), lambda b, c: (b, c, 0)),
                  pl.BlockSpec((1, rows, W_QK_R), lambda b, c: (b, c, 1)),
                  pl.BlockSpec((1, rows, W_V_R), lambda b, c: (b, c, 0)),
                  pl.BlockSpec((1, rows, W_V_R), lambda b, c: (b, c, 0)),
                  pl.BlockSpec((1, W_V_R), lambda b, c: (0, 0)),
                  pl.BlockSpec((1, H_R, DK_R, DV_R), lambda b, c: (b, 0, 0, 0))],
        out_specs=[pl.BlockSpec((1, rows, W_V_R), lambda b, c: (b, c, 0)),
                   pl.BlockSpec((1, H_R, DK_R, DV_R), lambda b, c: (b, 0, 0, 0))],
        out_shape=[jax.ShapeDtypeStruct((bx, tx, W_V_R), BF16),
                   jax.ShapeDtypeStruct((bx, H_R, DK_R, DV_R), F32)],
        compiler_params=_cparams(("parallel", "arbitrary")),
        name="retention",
    )(qk_r, qk_r, v_r, sg_r, gn_r.reshape(1, W_V_R), state0)


def _sb_step(qh, kb, vb, bias, tri, carry, acc, mask):
    z = _dot(qh, kb) + bias
    sp = jnp.log(1.0 + jnp.exp(-jnp.abs(z)))
    log_beta = jnp.minimum(z, 0.0) - sp
    log_1mb = log_beta - z
    if mask is not None:
        log_1mb = jnp.where(mask, log_1mb, 0.0)
    hi, lo = _split_bf16(log_1mb)
    after = _dot(hi, tri) + _dot(lo, tri) + carry
    a = jnp.exp(log_beta + after)
    if mask is not None:
        a = jnp.where(mask, a, 0.0)
    acc = acc + lax.dot_general(a.astype(BF16), vb, _NT, preferred_element_type=F32)
    carry = carry + jnp.sum(log_1mb, axis=1, keepdims=True)
    return carry, acc


def _tri(n):
    return (lax.broadcasted_iota(jnp.int32, (n, n), 0) > lax.broadcasted_iota(jnp.int32, (n, n), 1)).astype(BF16)


def _sbp_kernel(bias_ref, q_ref, k_ref, v_ref, sg_ref, o_ref, *, blk):
    hp = pl.program_id(1)
    i = pl.program_id(2)
    lane = lax.broadcasted_iota(jnp.int32, (1, 2 * HD_S), 1)
    q2 = q_ref[0]
    tri = _tri(blk)
    diag = (lax.broadcasted_iota(jnp.int32, (blk, blk), 1) < lax.broadcasted_iota(jnp.int32, (blk, blk), 0))
    qh = [jnp.where((lane // HD_S) == hh, q2, jnp.zeros_like(q2)) for hh in range(2)]
    bias = [bias_ref[hp * 2 + hh] for hh in range(2)]

    def fold(j, state, mask):
        kb = k_ref[0, j]
        vb = v_ref[0, j]
        out = []
        for hh in range(2):
            out.extend(_sb_step(qh[hh], kb, vb, bias[hh], tri, state[2 * hh], state[2 * hh + 1], mask))
        return tuple(out)

    zc = jnp.zeros((blk, 1), F32)
    za = jnp.zeros((blk, 2 * HD_S), F32)
    state = fold(i, (zc, za, zc, za), diag)
    state = lax.fori_loop(0, i, lambda jj, st: fold(i - 1 - jj, st, None), state)
    o = jnp.where(lane < HD_S, state[1], state[3])
    o_ref[0] = (o * sg_ref[0].astype(F32)).astype(o_ref.dtype)


def _sb_prompt(q_s, k_sb, v_sb, sg_s, sb_bias):
    bx, s, _ = q_s.shape
    nkb, blk = k_sb.shape[1], k_sb.shape[3]
    w2 = 2 * HD_S
    return pl.pallas_call(
        functools.partial(_sbp_kernel, blk=blk),
        grid=(bx, H_S // 2, s // blk),
        in_specs=[pl.BlockSpec(memory_space=pltpu.SMEM),
                  pl.BlockSpec((1, blk, w2), lambda b, hp, i: (b, i, hp)),
                  pl.BlockSpec((1, nkb, w2, blk), lambda b, hp, i: (b, 0, hp, 0)),
                  pl.BlockSpec((1, nkb, w2, blk), lambda b, hp, i: (b, 0, hp, 0)),
                  pl.BlockSpec((1, blk, w2), lambda b, hp, i: (b, i, hp))],
        out_specs=pl.BlockSpec((1, blk, w2), lambda b, hp, i: (b, i, hp)),
        out_shape=jax.ShapeDtypeStruct((bx, s, W_S), BF16),
        compiler_params=_cparams(("parallel", "parallel", "arbitrary")),
        name="sb_prompt",
    )(sb_bias, q_s, k_sb, v_sb, sg_s)


def _sbs_kernel(pt_ref, bias_ref, q_ref, kn_ref, vn_ref, sg_ref, *rest, npg, page, td):
    k_refs = rest[:npg]
    v_refs = rest[npg:2 * npg]
    o_ref = rest[2 * npg]
    qbd_ref, carry_ref, acc_ref = rest[2 * npg + 1:]
    g = pl.program_id(1)
    rows = H_S * td
    tri = _tri(page)
    bias = bias_ref[...]

    @pl.when(g == 0)
    def _():
        qt = jnp.concatenate([q_ref[0]] * H_S, axis=0)
        rh = lax.broadcasted_iota(jnp.int32, (rows, W_S), 0) // td
        lh = lax.broadcasted_iota(jnp.int32, (rows, W_S), 1) // HD_S
        qbd = jnp.where(rh == lh, qt, 0.0).astype(BF16)
        qbd_ref[...] = qbd
        kn = kn_ref[0].astype(BF16)
        vn = vn_ref[0].astype(BF16)
        t_of_row = lax.broadcasted_iota(jnp.int32, (rows, page), 0) % td
        mask = lax.broadcasted_iota(jnp.int32, (rows, page), 1) < t_of_row
        carry, acc = _sb_step(qbd, kn, vn, bias, tri, jnp.zeros((rows, 1), F32),
                              jnp.zeros((rows, W_S), F32), mask)
        carry_ref[...] = carry
        acc_ref[...] = acc

    qbd = qbd_ref[...]
    carry = carry_ref[...]
    for p in range(npg):
        kb = k_refs[p][0].astype(BF16)
        vb = v_refs[p][0].astype(BF16)
        carry, acc = _sb_step(qbd, kb, vb, bias, tri, carry, acc_ref[...], None)
        acc_ref[...] = acc
    carry_ref[...] = carry

    @pl.when(g == pl.num_programs(1) - 1)
    def _():
        acc = acc_ref[...]
        lh = lax.broadcasted_iota(jnp.int32, (td, W_S), 1) // HD_S
        out = jnp.zeros((td, W_S), F32)
        for h in range(H_S):
            out = jnp.where(lh == h, acc[h * td:(h + 1) * td, :], out)
        o_ref[0] = out * sg_ref[0]


def _sb_sample(q_s, k_new, v_new, sg_s, cache_k, cache_v, page_table, sb_bias):
    bd, td, _ = q_s.shape
    n_pages = page_table.shape[1]
    page = cache_k.shape[2]
    npg = min(PAGES_PER_STEP, n_pages)
    rows = H_S * td
    bias_col = jnp.repeat(sb_bias, td).reshape(rows, 1)

    def page_spec(p):
        return pl.BlockSpec((1, W_S, page), lambda b, g, pt: (pt[b, n_pages - 1 - (g * npg + p)], 0, 0))

    tok_spec = pl.BlockSpec((1, td, W_S), lambda b, g, pt: (b, 0, 0))
    new_spec = pl.BlockSpec((1, W_S, page), lambda b, g, pt: (b, 0, 0))
    grid_spec = pltpu.PrefetchScalarGridSpec(
        num_scalar_prefetch=1,
        grid=(bd, n_pages // npg),
        in_specs=[pl.BlockSpec((rows, 1), lambda b, g, pt: (0, 0)), tok_spec, new_spec, new_spec, tok_spec]
        + [page_spec(p) for p in range(npg)] * 2,
        out_specs=tok_spec,
        scratch_shapes=[pltpu.VMEM((rows, W_S), BF16), pltpu.VMEM((rows, 1), F32),
                        pltpu.VMEM((rows, W_S), F32)],
    )
    return pl.pallas_call(
        functools.partial(_sbs_kernel, npg=npg, page=page, td=td),
        grid_spec=grid_spec,
        out_shape=jax.ShapeDtypeStruct((bd, td, W_S), F32),
        compiler_params=_cparams(("parallel", "arbitrary")),
        name="sb_sample",
    )(page_table, bias_col, q_s, k_new, v_new, sg_s, *([cache_k] * npg), *([cache_v] * npg))


def _merge_kernel(ar_ref, wr_ref, as_ref, ws_ref, gr_ref, gs_ref, o_ref):
    y_r = _dot(ar_ref[0], wr_ref[...])
    y_s = _dot(as_ref[0], ws_ref[...])
    o_ref[0] = (gr_ref[0].astype(F32) * y_r + gs_ref[0].astype(F32) * y_s).astype(o_ref.dtype)


def _merge(br_r, br_s, gates, w_pr_b, w_ps_b, tm):
    bx, tx, _ = br_r.shape
    d = w_pr_b.shape[1]
    tn = 512
    nb = d // tn
    return pl.pallas_call(
        _merge_kernel,
        grid=(bx, tx // tm, nb),
        in_specs=[pl.BlockSpec((1, tm, W_V_R), lambda b, i, j: (b, i, 0)),
                  pl.BlockSpec((W_V_R, tn), lambda b, i, j: (0, j)),
                  pl.BlockSpec((1, tm, W_S), lambda b, i, j: (b, i, 0)),
                  pl.BlockSpec((W_S, tn), lambda b, i, j: (0, j)),
                  pl.BlockSpec((1, tm, tn), lambda b, i, j: (b, i, j)),
                  pl.BlockSpec((1, tm, tn), lambda b, i, j: (b, i, nb + j))],
        out_specs=pl.BlockSpec((1, tm, tn), lambda b, i, j: (b, i, j)),
        out_shape=jax.ShapeDtypeStruct((bx, tx, d), BF16),
        compiler_params=_cparams(("parallel", "parallel", "arbitrary")),
        name="merge",
    )(br_r, w_pr_b, br_s, w_ps_b, gates, gates)


def _out_kernel(m_ref, w_ref, x_ref, g_ref, o_ref):
    o_ref[0] = x_ref[0] + g_ref[0] * _dot(m_ref[0], w_ref[...])


def _out_proj(merged, w_out_b, x3, mod3, tm):
    bx, tx, d = x3.shape
    tn = 512
    return pl.pallas_call(
        _out_kernel,
        grid=(bx, tx // tm, d // tn),
        in_specs=[pl.BlockSpec((1, tm, d), lambda b, i, j: (b, i, 0)),
                  pl.BlockSpec((d, tn), lambda b, i, j: (0, j)),
                  pl.BlockSpec((1, tm, tn), lambda b, i, j: (b, i, j)),
                  _mod_spec(mod3, 2, tm, tn, d, 3)],
        out_specs=pl.BlockSpec((1, tm, tn), lambda b, i, j: (b, i, j)),
        out_shape=jax.ShapeDtypeStruct((bx, tx, d), F32),
        compiler_params=_cparams(("parallel", "parallel", "arbitrary")),
        name="out_proj",
    )(merged, w_out_b, x3, mod3)


def _rope_tables(pos):
    half = DK_R // 2
    inv_freq = ROPE_BASE ** (-jnp.arange(half, dtype=F32) / half)
    ang = pos.astype(F32)[:, None] * inv_freq[None, :]
    cos = jnp.cos(ang)
    sin = jnp.sin(ang)
    return jnp.concatenate([cos, cos], axis=1), jnp.concatenate([-sin, sin], axis=1)


def kernel(x_prompt, x_sample, cache_sb_k, cache_sb_v, state_ret, page_table, c_prompt, c_sample,
           norm_gain, w_ada, b_ada, w_in, qn_gain, kn_gain, sb_bias, gn_r, w_pr, w_ps, w_mg, b_mg, w_out):
    b, s, d = x_prompt.shape
    bd, td, _ = x_sample.shape
    n_pages = page_table.shape[1]
    page = cache_sb_k.shape[1]
    past = n_pages * page
    nd = bd * td
    assert s % SB_BLOCK == 0 or s < SB_BLOCK
    assert s % CHUNK == 0 and td <= 16 and page == LANES

    w_in_b, w_mg_b = w_in.astype(BF16), w_mg.astype(BF16)
    w_pr_b, w_ps_b, w_out_b = w_pr.astype(BF16), w_ps.astype(BF16), w_out.astype(BF16)

    mod = _ada(jnp.concatenate([c_prompt, c_sample], axis=0), w_ada, b_ada)
    mod_p = mod[:b].reshape(b, 1, 3 * d)
    mod_s = jnp.repeat(mod[b:], td, axis=0).reshape(1, nd, 3 * d)

    tm_p = min(1024, s)
    cos_p, sin_p = _rope_tables(jnp.arange(s))
    cos_s, sin_s = _rope_tables(jnp.tile(past + jnp.arange(td), bd))

    mp = _mixer_group(x_prompt, mod_p, cos_p, sin_p, norm_gain, w_in_b, w_mg_b, b_mg, qn_gain, kn_gain, tm_p, True)
    br_r, ret_state_prompt = _retention(mp["qk_r"], mp["v_r"], mp["sg_r"], gn_r,
                                        jnp.zeros((b, H_R, DK_R, DV_R), F32), CHUNK, CHUNK)
    br_s = _sb_prompt(mp["q_s"], mp["k_sb"], mp["v_sb"], mp["sg_s"], sb_bias)
    merged = _merge(br_r, br_s, mp["gates"], w_pr_b, w_ps_b, tm_p)
    y_p = _out_proj(merged, w_out_b, x_prompt, mod_p, tm_p)

    xs3 = x_sample.reshape(1, nd, d)
    ms = _mixer_group(xs3, mod_s, cos_s, sin_s, norm_gain, w_in_b, w_mg_b, b_mg, qn_gain, kn_gain, nd, False)
    rows = 16

    def pad_tok(a):
        a = a.reshape(bd, td, a.shape[-1])
        return jnp.pad(a, ((0, 0), (0, rows - td), (0, 0)))

    br_r2, ret_state_sample = _retention(pad_tok(ms["qk_r"]), pad_tok(ms["v_r"]), pad_tok(ms["sg_r"]), gn_r,
                                         state_ret, rows, td)
    br_r2 = br_r2[:, :td].reshape(1, nd, W_V_R)

    def tok_f32(a):
        return a.astype(F32).reshape(bd, td, W_S)

    def new_rows(a):
        return jnp.pad(a.reshape(bd, td, W_S).transpose(0, 2, 1), ((0, 0), (0, 0), (0, page - td)))

    def pages(c):
        return c.transpose(0, 2, 3, 1).reshape(-1, W_S, page)

    br_s2 = _sb_sample(tok_f32(ms["q_s"]), new_rows(ms["k_s"]), new_rows(ms["v_s"]), tok_f32(ms["sg_s"]),
                       pages(cache_sb_k), pages(cache_sb_v), page_table, sb_bias)
    br_s2 = br_s2.astype(BF16).reshape(1, nd, W_S)
    merged2 = _merge(br_r2, br_s2, ms["gates"], w_pr_b, w_ps_b, nd)
    y_s = _out_proj(merged2, w_out_b, xs3, mod_s, nd)

    def rows_layout(a):
        return a.reshape(b, H_S, HD_S, s).transpose(0, 3, 1, 2)

    return (y_p, y_s.reshape(bd, td, d), ret_state_prompt, ret_state_sample,
            rows_layout(mp["k_s"]), rows_layout(mp["v_s"]),
            ms["k_s"].reshape(bd, td, H_S, HD_S), ms["v_s"].reshape(bd, td, H_S, HD_S))
```

```python
import functools
import math

import jax
import jax.numpy as jnp
from jax import lax
from jax.experimental import pallas as pl
from jax.experimental.pallas import tpu as pltpu

F32 = jnp.float32
BF16 = jnp.bfloat16

H_R, DK_R, DV_R = 8, 128, 256
H_S, HD_S = 16, 64
W_QK_R = H_R * DK_R
W_V_R = H_R * DV_R
W_S = H_S * HD_S
CHUNK = 128
ROPE_BASE = 10000.0
EPS = 1e-6
LOG2E = math.log2(math.e)

LANES = 128
MXU_DIM = 256
VMEM_LIMIT = 48 * 1024 * 1024

SB_BLOCK = 256
SB_HEADS_PER_STEP = 8
PAGES_PER_STEP = 8

_NT = (((1,), (1,)), ((), ()))
_TN = (((0,), (0,)), ((), ()))


def _cparams(sem):
    return pltpu.CompilerParams(dimension_semantics=sem, vmem_limit_bytes=VMEM_LIMIT)


def _dot(a, b):
    return jnp.dot(a, b, preferred_element_type=F32)


def _split_bf16(x):
    hi = x.astype(BF16)
    lo = (x - hi.astype(F32)).astype(BF16)
    return hi, lo


def _ada_kernel(c_ref, w_ref, b_ref, o_ref):
    c = c_ref[...]
    a = c * jax.nn.sigmoid(c)
    o_ref[...] = jnp.dot(a, w_ref[...], preferred_element_type=F32,
                         precision=lax.Precision.HIGHEST) + b_ref[...]


def _ada(c_all, w_ada, b_ada):
    r, d = c_all.shape
    n = w_ada.shape[1]
    tn = 512
    return pl.pallas_call(
        _ada_kernel,
        grid=(n // tn,),
        in_specs=[pl.BlockSpec((r, d), lambda j: (0, 0)),
                  pl.BlockSpec((d, tn), lambda j: (0, j)),
                  pl.BlockSpec((1, tn), lambda j: (0, j))],
        out_specs=pl.BlockSpec((r, tn), lambda j: (0, j)),
        out_shape=jax.ShapeDtypeStruct((r, n), F32),
        compiler_params=_cparams(("arbitrary",)),
        name="ada",
    )(c_all, w_ada, b_ada.reshape(1, n))


def _prep_kernel(x_ref, g_ref, sh_ref, sc_ref, h_ref):
    x = x_ref[0]
    inv = lax.rsqrt(jnp.mean(x * x, axis=-1, keepdims=True) + EPS)
    h = (x * inv) * g_ref[...] * (1.0 + sc_ref[0]) + sh_ref[0]
    h_ref[0] = h.astype(BF16)


def _mod_spec(mod3, part, tm, tn, d, ngrid):
    per_row = mod3.shape[1] != 1
    nb = d // tn
    if ngrid == 2:
        if per_row:
            return pl.BlockSpec((1, tm, tn), lambda b, i: (b, i, part * nb))
        return pl.BlockSpec((1, 1, tn), lambda b, i: (b, 0, part * nb))
    if per_row:
        return pl.BlockSpec((1, tm, tn), lambda b, i, j: (b, i, part * nb + j))
    return pl.BlockSpec((1, 1, tn), lambda b, i, j: (b, 0, part * nb + j))


def _prep(x3, mod3, norm_gain, tm):
    bx, tx, d = x3.shape
    return pl.pallas_call(
        _prep_kernel,
        grid=(bx, tx // tm),
        in_specs=[pl.BlockSpec((1, tm, d), lambda b, i: (b, i, 0)),
                  pl.BlockSpec((1, d), lambda b, i: (0, 0)),
                  _mod_spec(mod3, 0, tm, d, d, 2),
                  _mod_spec(mod3, 1, tm, d, d, 2)],
        out_specs=pl.BlockSpec((1, tm, d), lambda b, i: (b, i, 0)),
        out_shape=jax.ShapeDtypeStruct((bx, tx, d), BF16),
        compiler_params=_cparams(("parallel", "parallel")),
        name="prep",
    )(x3, norm_gain.reshape(1, d), mod3, mod3)


def _mm_kernel(a_ref, w_ref, *rest, epi):
    acc = _dot(a_ref[0], w_ref[...])
    epi(acc, *rest)


def _mm(a3, w, col0, n, epi, tm, tn, extras, extra_specs, out_dtypes, name):
    bx, tx, k = a3.shape
    cb0 = col0 // tn
    out_shape = [jax.ShapeDtypeStruct((bx, tx, n), dt) for dt in out_dtypes]
    out_specs = [pl.BlockSpec((1, tm, tn), lambda b, i, j: (b, i, j)) for _ in out_dtypes]
    return pl.pallas_call(
        functools.partial(_mm_kernel, epi=epi),
        grid=(bx, tx // tm, n // tn),
        in_specs=[pl.BlockSpec((1, tm, k), lambda b, i, j: (b, i, 0)),
                  pl.BlockSpec((k, tn), lambda b, i, j: (0, cb0 + j))] + list(extra_specs),
        out_specs=out_specs,
        out_shape=out_shape,
        compiler_params=_cparams(("parallel", "parallel", "arbitrary")),
        name=name,
    )(a3, w, *extras)


def _epi_rot(acc, cos_ref, sin_ref, o_ref, *, tn):
    j = pl.program_id(2)
    scale = jnp.where(j * tn >= W_QK_R, DK_R ** -0.5, 1.0).astype(F32)
    cos = cos_ref[...]
    sin = sin_ref[...]
    for hh in range(tn // DK_R):
        xh = acc[:, hh * DK_R:(hh + 1) * DK_R]
        r = pltpu.roll(xh, DK_R // 2, axis=1)
        o_ref[0, :, hh * DK_R:(hh + 1) * DK_R] = ((xh * cos + r * sin) * scale).astype(o_ref.dtype)


def _epi_plain(acc, o_ref):
    o_ref[0] = acc.astype(o_ref.dtype)


def _epi_silu(acc, o_ref):
    o_ref[0] = (acc * jax.nn.sigmoid(acc)).astype(o_ref.dtype)


def _head_norm(acc, g_ref, gain_ref, tn):
    g = g_ref[...]
    parts = []
    for c in range(tn // MXU_DIM):
        a = acc[:, c * MXU_DIM:(c + 1) * MXU_DIM]
        hi, lo = _split_bf16(a * a)
        ms = _dot(hi, g) + _dot(lo, g)
        parts.append(a * lax.rsqrt(ms + EPS))
    y = parts[0] if len(parts) == 1 else jnp.concatenate(parts, axis=1)
    return y * gain_ref[...]


def _epi_qs(acc, g_ref, gain_ref, o_ref, *, tn):
    o_ref[0] = (_head_norm(acc, g_ref, gain_ref, tn) * HD_S ** -0.5).astype(o_ref.dtype)


def _epi_ks1(acc, g_ref, gain_ref, o_ref, *, tn):
    o_ref[0] = _head_norm(acc, g_ref, gain_ref, tn)


def _epi_gates(acc, b_ref, o_ref):
    o_ref[0] = jax.nn.sigmoid(acc + b_ref[...]).astype(o_ref.dtype)


def _mmt_kernel(a_ref, wt_ref, *rest, norm, tn, blk):
    if norm:
        g_ref, gain_ref, o_ref, ob_ref = rest
    else:
        o_ref, ob_ref = rest
    acc = lax.dot_general(wt_ref[...], a_ref[0], _NT, preferred_element_type=F32)
    if norm:
        g = g_ref[...]
        parts = []
        for c in range(tn // MXU_DIM):
            a = acc[c * MXU_DIM:(c + 1) * MXU_DIM, :]
            hi, lo = _split_bf16(a * a)
            ms = _dot(g, hi) + _dot(g, lo)
            parts.append(a * lax.rsqrt(ms + EPS))
        acc = (parts[0] if len(parts) == 1 else jnp.concatenate(parts, axis=0)) * gain_ref[...]
    o_ref[0] = acc
    for kb in range(ob_ref.shape[1]):
        ob_ref[0, kb] = acc[:, kb * blk:(kb + 1) * blk].astype(ob_ref.dtype)


def _mm_t(a3, wt, norm_inputs, tm, tn, blk, name):
    bx, tx, k = a3.shape
    n = wt.shape[0]
    norm = norm_inputs is not None
    extra, extra_specs = (), []
    if norm:
        extra = norm_inputs
        extra_specs = [pl.BlockSpec((MXU_DIM, MXU_DIM), lambda b, i, j: (0, 0)),
                       pl.BlockSpec((tn, 1), lambda b, i, j: (j, 0))]
    return pl.pallas_call(
        functools.partial(_mmt_kernel, norm=norm, tn=tn, blk=blk),
        grid=(bx, tx // tm, n // tn),
        in_specs=[pl.BlockSpec((1, tm, k), lambda b, i, j: (b, i, 0)),
                  pl.BlockSpec((tn, k), lambda b, i, j: (j, 0))] + extra_specs,
        out_specs=[pl.BlockSpec((1, tn, tm), lambda b, i, j: (b, j, i)),
                   pl.BlockSpec((1, tm // blk, tn, blk), lambda b, i, j: (b, i, j, 0))],
        out_shape=[jax.ShapeDtypeStruct((bx, n, tx), F32),
                   jax.ShapeDtypeStruct((bx, tx // blk, n, blk), BF16)],
        compiler_params=_cparams(("parallel", "parallel", "arbitrary")),
        name=name,
    )(a3, wt, *extra)


def _mixer_group(x3, mod3, cos_t, sin_t, norm_gain, w_in_b, w_mg_b, b_mg, qn_gain, kn_gain, tm, feature_major_kv):
    d = x3.shape[2]
    tn = 512
    h = _prep(x3, mod3, norm_gain, tm)
    avg = jnp.where((jnp.arange(MXU_DIM)[:, None] // HD_S) == (jnp.arange(MXU_DIM)[None, :] // HD_S),
                    1.0 / HD_S, 0.0).astype(BF16)
    avg_spec = pl.BlockSpec((MXU_DIM, MXU_DIM), lambda b, i, j: (0, 0))
    gain_spec = pl.BlockSpec((1, tn), lambda b, i, j: (0, j))
    tab_spec = pl.BlockSpec((tm, DK_R), lambda b, i, j: (i, 0))

    c0 = 0
    (qk_r,) = _mm(h, w_in_b, c0, 2 * W_QK_R, functools.partial(_epi_rot, tn=tn), tm, tn,
                  (cos_t, sin_t), (tab_spec, tab_spec), (BF16,), "proj_qk_r")
    c0 += 2 * W_QK_R
    (v_r,) = _mm(h, w_in_b, c0, W_V_R, _epi_plain, tm, tn, (), (), (BF16,), "proj_v_r")
    c0 += W_V_R
    (sg_r,) = _mm(h, w_in_b, c0, W_V_R, _epi_silu, tm, tn, (), (), (BF16,), "proj_g_r")
    c0 += W_V_R
    (q_s,) = _mm(h, w_in_b, c0, W_S, functools.partial(_epi_qs, tn=tn), tm, tn,
                 (avg, qn_gain.reshape(1, W_S)), (avg_spec, gain_spec), (BF16,), "proj_q_s")
    c0 += W_S
    if feature_major_kv:
        blk = min(SB_BLOCK, tm)
        k_s, k_sb = _mm_t(h, w_in_b[:, c0:c0 + W_S].T, (avg, kn_gain.reshape(W_S, 1)), tm, tn, blk, "proj_k_s_t")
        c0 += W_S
        v_s, v_sb = _mm_t(h, w_in_b[:, c0:c0 + W_S].T, None, tm, tn, blk, "proj_v_s_t")
        c0 += W_S
    else:
        (k_s,) = _mm(h, w_in_b, c0, W_S, functools.partial(_epi_ks1, tn=tn), tm, tn,
                     (avg, kn_gain.reshape(1, W_S)), (avg_spec, gain_spec), (F32,), "proj_k_s")
        c0 += W_S
        (v_s,) = _mm(h, w_in_b, c0, W_S, _epi_plain, tm, tn, (), (), (F32,), "proj_v_s")
        c0 += W_S
        k_sb = v_sb = None
    (sg_s,) = _mm(h, w_in_b, c0, W_S, _epi_silu, tm, tn, (), (), (BF16,), "proj_g_s")
    (gates,) = _mm(h, w_mg_b, 0, 2 * d, _epi_gates, tm, tn,
                   (b_mg.reshape(1, 2 * d),), (gain_spec,), (BF16,), "proj_gates")
    return dict(qk_r=qk_r, v_r=v_r, sg_r=sg_r, q_s=q_s, k_s=k_s, k_sb=k_sb, v_s=v_s, v_sb=v_sb,
                sg_s=sg_s, gates=gates)


def _ret_kernel(q_ref, k_ref, v_ref, sg_ref, gn_ref, s0_ref, o_ref, st_ref, *, rows, chunk):
    @pl.when(pl.program_id(1) == 0)
    def _():
        st_ref[...] = s0_ref[...]

    ri = lax.broadcasted_iota(jnp.int32, (rows, rows), 0)
    ci = lax.broadcasted_iota(jnp.int32, (rows, rows), 1)
    diff = (ri - ci).astype(F32)
    idx = lax.broadcasted_iota(jnp.int32, (rows, 1), 0).astype(F32)
    for h in range(H_R):
        lg = math.log(1.0 - 2.0 ** (-5.0 - h))
        decay = jnp.where(diff >= 0.0, jnp.exp(lg * jnp.maximum(diff, 0.0)), 0.0)
        q_dec = jnp.exp(lg * (idx + 1.0))
        k_dec = jnp.exp(lg * (chunk - 1.0 - idx))
        q = q_ref[0, :, h * DK_R:(h + 1) * DK_R]
        k = k_ref[0, :, h * DK_R:(h + 1) * DK_R]
        v = v_ref[0, :, h * DV_R:(h + 1) * DV_R]
        st = st_ref[0, h]
        scores = lax.dot_general(q, k, _NT, preferred_element_type=F32) * decay
        inner = _dot(scores.astype(BF16), v)
        cross = _dot(q, st.astype(BF16)) * q_dec
        kd = (k.astype(F32) * k_dec).astype(BF16)
        st_ref[0, h] = math.exp(lg * chunk) * st + lax.dot_general(kd, v, _TN, preferred_element_type=F32)
        o = inner + cross
        inv = lax.rsqrt(jnp.mean(o * o, axis=-1, keepdims=True) + EPS)
        sl = slice(h * DV_R, (h + 1) * DV_R)
        o_ref[0, :, sl] = ((o * inv) * gn_ref[:, sl] * sg_ref[0, :, sl].astype(F32)).astype(o_ref.dtype)


def _retention(qk_r, v_r, sg_r, gn_r, state0, rows, chunk):
    bx, tx, _ = v_r.shape
    nc = tx // rows
    return pl.pallas_call(
        functools.partial(_ret_kernel, rows=rows, chunk=chunk),
        grid=(bx, nc),
        in_specs=[pl.BlockSpec((1, rows, W_QK_R), lambda b, c: (b, c, 0)),
                  pl.BlockSpec((1, rows, W_QK_R), lambda b, c: (b, c, 1)),
                  pl.BlockSpec((1, rows, W_V_R), lambda b, c: (b, c, 0)),
                  pl.BlockSpec((1, rows, W_V_R), lambda b, c: (b, c, 0)),
                  pl.BlockSpec((1, W_V_R), lambda b, c: (0, 0)),
                  pl.BlockSpec((1, H_R, DK_R, DV_R), lambda b, c: (b, 0, 0, 0))],
        out_specs=[pl.BlockSpec((1, rows, W_V_R), lambda b, c: (b, c, 0)),
                   pl.BlockSpec((1, H_R, DK_R, DV_R), lambda b, c: (b, 0, 0, 0))],
        out_shape=[jax.ShapeDtypeStruct((bx, tx, W_V_R), BF16),
                   jax.ShapeDtypeStruct((bx, H_R, DK_R, DV_R), F32)],
        compiler_params=_cparams(("parallel", "arbitrary")),
        name="retention",
    )(qk_r, qk_r, v_r, sg_r, gn_r.reshape(1, W_V_R), state0)


def _sb_gate(z, tri, mask):
    sp = jnp.log(1.0 + jnp.exp2(jnp.abs(z) * -LOG2E))
    log_beta = jnp.minimum(z, 0.0) - sp
    log_1mb = log_beta - z
    if mask is not None:
        log_1mb = jnp.where(mask, log_1mb, 0.0)
    c = tri.shape[0]
    parts, sums = [], []
    for ci in range(z.shape[1] // c):
        part = log_1mb[:, ci * c:(ci + 1) * c]
        parts.append(jnp.concatenate(_split_bf16(part), axis=1))
        sums.append(jnp.sum(part, axis=1, keepdims=True))
    return log_beta, parts, sums


def _sb_suffix(parts, tri):
    tri2 = jnp.concatenate([tri, tri], axis=0)
    return [_dot(p, tri2) for p in parts]


def _sb_weights(log_beta, local, sums, carry, mask):
    afters = [None] * len(local)
    for ci in reversed(range(len(local))):
        afters[ci] = local[ci] + carry
        carry = carry + sums[ci]
    after = afters[0] if len(afters) == 1 else jnp.concatenate(afters, axis=1)
    a = jnp.exp(log_beta + after)
    if mask is not None:
        a = jnp.where(mask, a, 0.0)
    return a.astype(BF16), carry


def _sb_values(a, vb):
    return lax.dot_general(a, vb, _NT, preferred_element_type=F32)


def _sb_step(qh, kb, vb, bias, tri, carry, mask):
    log_beta, parts, sums = _sb_gate(_dot(qh, kb) + bias, tri, mask)
    a, carry = _sb_weights(log_beta, _sb_suffix(parts, tri), sums, carry, mask)
    return carry, _sb_values(a, vb)


def _tri(n):
    return (lax.broadcasted_iota(jnp.int32, (n, n), 0) > lax.broadcasted_iota(jnp.int32, (n, n), 1)).astype(BF16)


def _sbp_kernel(bias_ref, q_ref, k_ref, v_ref, sg_ref, o_ref, qh_ref, carry_ref, acc_ref, *, blk, nh):
    hg = pl.program_id(1)
    i = pl.program_id(2)
    w2 = 2 * HD_S
    lane = lax.broadcasted_iota(jnp.int32, (1, w2), 1)
    tri = _tri(blk)
    diag = (lax.broadcasted_iota(jnp.int32, (blk, blk), 1) < lax.broadcasted_iota(jnp.int32, (blk, blk), 0))
    for h in range(nh):
        q2 = q_ref[0, :, (h // 2) * w2:(h // 2 + 1) * w2]
        qh_ref[h] = jnp.where((lane // HD_S) == h % 2, q2, jnp.zeros_like(q2))
    bias = [bias_ref[hg * nh + h] for h in range(nh)]
    carry_ref[...] = jnp.zeros_like(carry_ref)
    acc_ref[...] = jnp.zeros_like(acc_ref)

    def fold(j, mask):
        logits, gates, local, weights = {}, {}, {}, {}

        def rows(h):
            return slice((h // 2) * w2, (h // 2 + 1) * w2)

        for n in range(nh + 4):
            if n < nh:
                logits[n] = _dot(qh_ref[n], k_ref[0, j, rows(n), :])
            h = n - 2
            if 0 <= h < nh:
                local[h] = _sb_suffix(gates[h][1], tri)
            h = n - 4
            if 0 <= h < nh:
                acc_ref[h] += _sb_values(weights.pop(h), v_ref[0, j, rows(h), :])
            h = n - 1
            if 0 <= h < nh:
                gates[h] = _sb_gate(logits.pop(h) + bias[h], tri, mask)
            h = n - 3
            if 0 <= h < nh:
                log_beta, _, sums = gates.pop(h)
                weights[h], carry_ref[h] = _sb_weights(log_beta, local.pop(h), sums, carry_ref[h], mask)

    fold(i, diag)

    def body(jj, c):
        fold(i - 1 - jj, None)
        return c

    lax.fori_loop(0, i, body, 0)
    for p in range(nh // 2):
        o = jnp.where(lane < HD_S, acc_ref[2 * p], acc_ref[2 * p + 1])
        sl = slice(p * w2, (p + 1) * w2)
        o_ref[0, :, sl] = (o * sg_ref[0, :, sl].astype(F32)).astype(o_ref.dtype)


def _sb_prompt(q_s, k_sb, v_sb, sg_s, sb_bias):
    bx, s, _ = q_s.shape
    nkb, blk = k_sb.shape[1], k_sb.shape[3]
    nh = SB_HEADS_PER_STEP
    w2 = nh * HD_S
    return pl.pallas_call(
        functools.partial(_sbp_kernel, blk=blk, nh=nh),
        grid=(bx, H_S // nh, s // blk),
        in_specs=[pl.BlockSpec(memory_space=pltpu.SMEM),
                  pl.BlockSpec((1, blk, w2), lambda b, hp, i: (b, i, hp)),
                  pl.BlockSpec((1, nkb, w2, blk), lambda b, hp, i: (b, 0, hp, 0)),
                  pl.BlockSpec((1, nkb, w2, blk), lambda b, hp, i: (b, 0, hp, 0)),
                  pl.BlockSpec((1, blk, w2), lambda b, hp, i: (b, i, hp))],
        out_specs=pl.BlockSpec((1, blk, w2), lambda b, hp, i: (b, i, hp)),
        out_shape=jax.ShapeDtypeStruct((bx, s, W_S), BF16),
        scratch_shapes=[pltpu.VMEM((nh, blk, 2 * HD_S), BF16), pltpu.VMEM((nh, blk, 1), F32),
                        pltpu.VMEM((nh, blk, 2 * HD_S), F32)],
        compiler_params=_cparams(("parallel", "parallel", "arbitrary")),
        name="sb_prompt",
    )(sb_bias, q_s, k_sb, v_sb, sg_s)


def _sbs_kernel(pt_ref, bias_ref, q_ref, kn_ref, vn_ref, sg_ref, *rest, npg, page, td):
    k_refs = rest[:npg]
    v_refs = rest[npg:2 * npg]
    o_ref = rest[2 * npg]
    qbd_ref, carry_ref, acc_ref = rest[2 * npg + 1:]
    g = pl.program_id(1)
    rows = H_S * td
    tri = _tri(MXU_DIM if (npg * page) % MXU_DIM == 0 else page)
    bias = bias_ref[...]

    @pl.when(g == 0)
    def _():
        qt = jnp.concatenate([q_ref[0]] * H_S, axis=0)
        rh = lax.broadcasted_iota(jnp.int32, (rows, W_S), 0) // td
        lh = lax.broadcasted_iota(jnp.int32, (rows, W_S), 1) // HD_S
        qbd = jnp.where(rh == lh, qt, 0.0).astype(BF16)
        qbd_ref[...] = qbd
        kn = kn_ref[0].astype(BF16)
        vn = vn_ref[0].astype(BF16)
        t_of_row = lax.broadcasted_iota(jnp.int32, (rows, page), 0) % td
        mask = lax.broadcasted_iota(jnp.int32, (rows, page), 1) < t_of_row
        carry, acc = _sb_step(qbd, kn, vn, bias, _tri(page), jnp.zeros((rows, 1), F32), mask)
        carry_ref[...] = carry
        acc_ref[...] = acc

    kb = jnp.concatenate([k_refs[p][0].astype(BF16) for p in reversed(range(npg))], axis=1)
    vb = jnp.concatenate([v_refs[p][0].astype(BF16) for p in reversed(range(npg))], axis=1)
    carry, contrib = _sb_step(qbd_ref[...], kb, vb, bias, tri, carry_ref[...], None)
    acc_ref[...] += contrib
    carry_ref[...] = carry

    @pl.when(g == pl.num_programs(1) - 1)
    def _():
        acc = acc_ref[...]
        lh = lax.broadcasted_iota(jnp.int32, (td, W_S), 1) // HD_S
        out = jnp.zeros((td, W_S), F32)
        for h in range(H_S):
            out = jnp.where(lh == h, acc[h * td:(h + 1) * td, :], out)
        o_ref[0] = out * sg_ref[0]


def _sb_sample(q_s, k_new, v_new, sg_s, cache_k, cache_v, page_table, sb_bias):
    bd, td, _ = q_s.shape
    n_pages = page_table.shape[1]
    page = cache_k.shape[2]
    npg = min(PAGES_PER_STEP, n_pages)
    rows = H_S * td
    bias_col = jnp.repeat(sb_bias, td).reshape(rows, 1)

    def page_spec(p):
        return pl.BlockSpec((1, W_S, page), lambda b, g, pt: (pt[b, n_pages - 1 - (g * npg + p)], 0, 0))

    tok_spec = pl.BlockSpec((1, td, W_S), lambda b, g, pt: (b, 0, 0))
    new_spec = pl.BlockSpec((1, W_S, page), lambda b, g, pt: (b, 0, 0))
    grid_spec = pltpu.PrefetchScalarGridSpec(
        num_scalar_prefetch=1,
        grid=(bd, n_pages // npg),
        in_specs=[pl.BlockSpec((rows, 1), lambda b, g, pt: (0, 0)), tok_spec, new_spec, new_spec, tok_spec]
        + [page_spec(p) for p in range(npg)] * 2,
        out_specs=tok_spec,
        scratch_shapes=[pltpu.VMEM((rows, W_S), BF16), pltpu.VMEM((rows, 1), F32),
                        pltpu.VMEM((rows, W_S), F32)],
    )
    return pl.pallas_call(
        functools.partial(_sbs_kernel, npg=npg, page=page, td=td),
        grid_spec=grid_spec,
        out_shape=jax.ShapeDtypeStruct((bd, td, W_S), F32),
        compiler_params=_cparams(("parallel", "arbitrary")),
        name="sb_sample",
    )(page_table, bias_col, q_s, k_new, v_new, sg_s, *([cache_k] * npg), *([cache_v] * npg))


def _merge_kernel(ar_ref, wr_ref, as_ref, ws_ref, gr_ref, gs_ref, o_ref):
    y_r = _dot(ar_ref[0], wr_ref[...])
    y_s = _dot(as_ref[0], ws_ref[...])
    o_ref[0] = (gr_ref[0].astype(F32) * y_r + gs_ref[0].astype(F32) * y_s).astype(o_ref.dtype)


def _merge(br_r, br_s, gates, w_pr_b, w_ps_b, tm):
    bx, tx, _ = br_r.shape
    d = w_pr_b.shape[1]
    tn = 512
    nb = d // tn
    return pl.pallas_call(
        _merge_kernel,
        grid=(bx, tx // tm, nb),
        in_specs=[pl.BlockSpec((1, tm, W_V_R), lambda b, i, j: (b, i, 0)),
                  pl.BlockSpec((W_V_R, tn), lambda b, i, j: (0, j)),
                  pl.BlockSpec((1, tm, W_S), lambda b, i, j: (b, i, 0)),
                  pl.BlockSpec((W_S, tn), lambda b, i, j: (0, j)),
                  pl.BlockSpec((1, tm, tn), lambda b, i, j: (b, i, j)),
                  pl.BlockSpec((1, tm, tn), lambda b, i, j: (b, i, nb + j))],
        out_specs=pl.BlockSpec((1, tm, tn), lambda b, i, j: (b, i, j)),
        out_shape=jax.ShapeDtypeStruct((bx, tx, d), BF16),
        compiler_params=_cparams(("parallel", "parallel", "arbitrary")),
        name="merge",
    )(br_r, w_pr_b, br_s, w_ps_b, gates, gates)


def _out_kernel(m_ref, w_ref, x_ref, g_ref, o_ref):
    o_ref[0] = x_ref[0] + g_ref[0] * _dot(m_ref[0], w_ref[...])


def _out_proj(merged, w_out_b, x3, mod3, tm):
    bx, tx, d = x3.shape
    tn = 512
    return pl.pallas_call(
        _out_kernel,
        grid=(bx, tx // tm, d // tn),
        in_specs=[pl.BlockSpec((1, tm, d), lambda b, i, j: (b, i, 0)),
                  pl.BlockSpec((d, tn), lambda b, i, j: (0, j)),
                  pl.BlockSpec((1, tm, tn), lambda b, i, j: (b, i, j)),
                  _mod_spec(mod3, 2, tm, tn, d, 3)],
        out_specs=pl.BlockSpec((1, tm, tn), lambda b, i, j: (b, i, j)),
        out_shape=jax.ShapeDtypeStruct((bx, tx, d), F32),
        compiler_params=_cparams(("parallel", "parallel", "arbitrary")),
        name="out_proj",
    )(merged, w_out_b, x3, mod3)


def _rope_tables(pos):
    half = DK_R // 2
    inv_freq = ROPE_BASE ** (-jnp.arange(half, dtype=F32) / half)
    ang = pos.astype(F32)[:, None] * inv_freq[None, :]
    cos = jnp.cos(ang)
    sin = jnp.sin(ang)
    return jnp.concatenate([cos, cos], axis=1), jnp.concatenate([-sin, sin], axis=1)


def kernel(x_prompt, x_sample, cache_sb_k, cache_sb_v, state_ret, page_table, c_prompt, c_sample,
           norm_gain, w_ada, b_ada, w_in, qn_gain, kn_gain, sb_bias, gn_r, w_pr, w_ps, w_mg, b_mg, w_out):
    b, s, d = x_prompt.shape
    bd, td, _ = x_sample.shape
    n_pages = page_table.shape[1]
    page = cache_sb_k.shape[1]
    past = n_pages * page
    nd = bd * td
    assert s % SB_BLOCK == 0 or s < SB_BLOCK
    assert s % CHUNK == 0 and td <= 16 and page == LANES

    w_in_b, w_mg_b = w_in.astype(BF16), w_mg.astype(BF16)
    w_pr_b, w_ps_b, w_out_b = w_pr.astype(BF16), w_ps.astype(BF16), w_out.astype(BF16)

    mod = _ada(jnp.concatenate([c_prompt, c_sample], axis=0), w_ada, b_ada)
    mod_p = mod[:b].reshape(b, 1, 3 * d)
    mod_s = jnp.repeat(mod[b:], td, axis=0).reshape(1, nd, 3 * d)

    tm_p = min(1024, s)
    cos_p, sin_p = _rope_tables(jnp.arange(s))
    cos_s, sin_s = _rope_tables(jnp.tile(past + jnp.arange(td), bd))

    mp = _mixer_group(x_prompt, mod_p, cos_p, sin_p, norm_gain, w_in_b, w_mg_b, b_mg, qn_gain, kn_gain, tm_p, True)
    br_r, ret_state_prompt = _retention(mp["qk_r"], mp["v_r"], mp["sg_r"], gn_r,
                                        jnp.zeros((b, H_R, DK_R, DV_R), F32), CHUNK, CHUNK)
    br_s = _sb_prompt(mp["q_s"], mp["k_sb"], mp["v_sb"], mp["sg_s"], sb_bias)
    merged = _merge(br_r, br_s, mp["gates"], w_pr_b, w_ps_b, tm_p)
    y_p = _out_proj(merged, w_out_b, x_prompt, mod_p, tm_p)

    xs3 = x_sample.reshape(1, nd, d)
    ms = _mixer_group(xs3, mod_s, cos_s, sin_s, norm_gain, w_in_b, w_mg_b, b_mg, qn_gain, kn_gain, nd, False)
    rows = 16

    def pad_tok(a):
        a = a.reshape(bd, td, a.shape[-1])
        return jnp.pad(a, ((0, 0), (0, rows - td), (0, 0)))

    br_r2, ret_state_sample = _retention(pad_tok(ms["qk_r"]), pad_tok(ms["v_r"]), pad_tok(ms["sg_r"]), gn_r,
                                         state_ret, rows, td)
    br_r2 = br_r2[:, :td].reshape(1, nd, W_V_R)

    def tok_f32(a):
        return a.astype(F32).reshape(bd, td, W_S)

    def new_rows(a):
        return jnp.pad(a.reshape(bd, td, W_S).transpose(0, 2, 1), ((0, 0), (0, 0), (0, page - td)))

    def pages(c):
        return c.transpose(0, 2, 3, 1).reshape(-1, W_S, page)

    br_s2 = _sb_sample(tok_f32(ms["q_s"]), new_rows(ms["k_s"]), new_rows(ms["v_s"]), tok_f32(ms["sg_s"]),
                       pages(cache_sb_k), pages(cache_sb_v), page_table, sb_bias)
    br_s2 = br_s2.astype(BF16).reshape(1, nd, W_S)
    merged2 = _merge(br_r2, br_s2, ms["gates"], w_pr_b, w_ps_b, nd)
    y_s = _out_proj(merged2, w_out_b, xs3, mod_s, nd)

    def rows_layout(a):
        return a.reshape(b, H_S, HD_S, s).transpose(0, 3, 1, 2)

    return (y_p, y_s.reshape(bd, td, d), ret_state_prompt, ret_state_sample,
            rows_layout(mp["k_s"]), rows_layout(mp["v_s"]),
            ms["k_s"].reshape(bd, td, H_S, HD_S), ms["v_s"].reshape(bd, td, H_S, HD_S))
```

```python
import functools
import math

import jax
import jax.numpy as jnp
from jax import lax
from jax.experimental import pallas as pl
from jax.experimental.pallas import tpu as pltpu

F32 = jnp.float32
BF16 = jnp.bfloat16

H_R, DK_R, DV_R = 8, 128, 256
H_S, HD_S = 16, 64
W_QK_R = H_R * DK_R
W_V_R = H_R * DV_R
W_S = H_S * HD_S
CHUNK = 128
ROPE_BASE = 10000.0
EPS = 1e-6
LOG2E = math.log2(math.e)

LANES = 128
MXU_DIM = 256
VMEM_LIMIT = 48 * 1024 * 1024
VMEM_LIMIT_PAGED = 56 * 1024 * 1024

PROJ_TN = 1024
SB_BLOCK = 256
SB_HEADS_PER_STEP = 8
PAGES_PER_STEP = 16

_NT = (((1,), (1,)), ((), ()))
_TN = (((0,), (0,)), ((), ()))


def _cparams(sem, vmem=VMEM_LIMIT):
    return pltpu.CompilerParams(dimension_semantics=sem, vmem_limit_bytes=vmem)


def _dot(a, b):
    return jnp.dot(a, b, preferred_element_type=F32)


def _split_bf16(x):
    hi = x.astype(BF16)
    lo = (x - hi.astype(F32)).astype(BF16)
    return hi, lo


def _ada_kernel(c_ref, w_ref, b_ref, o_ref):
    c = c_ref[...]
    a = c * jax.nn.sigmoid(c)
    o_ref[...] = jnp.dot(a, w_ref[...], preferred_element_type=F32,
                         precision=lax.Precision.HIGHEST) + b_ref[...]


def _ada(c_all, w_ada, b_ada):
    r, d = c_all.shape
    n = w_ada.shape[1]
    tn = 512
    return pl.pallas_call(
        _ada_kernel,
        grid=(n // tn,),
        in_specs=[pl.BlockSpec((r, d), lambda j: (0, 0)),
                  pl.BlockSpec((d, tn), lambda j: (0, j)),
                  pl.BlockSpec((1, tn), lambda j: (0, j))],
        out_specs=pl.BlockSpec((r, tn), lambda j: (0, j)),
        out_shape=jax.ShapeDtypeStruct((r, n), F32),
        compiler_params=_cparams(("arbitrary",)),
        name="ada",
    )(c_all, w_ada, b_ada.reshape(1, n))


def _prep_kernel(x_ref, g_ref, sh_ref, sc_ref, h_ref):
    x = x_ref[0]
    inv = lax.rsqrt(jnp.mean(x * x, axis=-1, keepdims=True) + EPS)
    h = (x * inv) * g_ref[...] * (1.0 + sc_ref[0]) + sh_ref[0]
    h_ref[0] = h.astype(BF16)


def _mod_spec(mod3, part, tm, tn, d, ngrid):
    per_row = mod3.shape[1] != 1
    nb = d // tn
    if ngrid == 2:
        if per_row:
            return pl.BlockSpec((1, tm, tn), lambda b, i: (b, i, part * nb))
        return pl.BlockSpec((1, 1, tn), lambda b, i: (b, 0, part * nb))
    if per_row:
        return pl.BlockSpec((1, tm, tn), lambda b, i, j: (b, i, part * nb + j))
    return pl.BlockSpec((1, 1, tn), lambda b, i, j: (b, 0, part * nb + j))


def _prep(x3, mod3, norm_gain, tm):
    bx, tx, d = x3.shape
    return pl.pallas_call(
        _prep_kernel,
        grid=(bx, tx // tm),
        in_specs=[pl.BlockSpec((1, tm, d), lambda b, i: (b, i, 0)),
                  pl.BlockSpec((1, d), lambda b, i: (0, 0)),
                  _mod_spec(mod3, 0, tm, d, d, 2),
                  _mod_spec(mod3, 1, tm, d, d, 2)],
        out_specs=pl.BlockSpec((1, tm, d), lambda b, i: (b, i, 0)),
        out_shape=jax.ShapeDtypeStruct((bx, tx, d), BF16),
        compiler_params=_cparams(("parallel", "parallel")),
        name="prep",
    )(x3, norm_gain.reshape(1, d), mod3, mod3)


def _mm_kernel(a_ref, w_ref, *rest, epi):
    acc = _dot(a_ref[0], w_ref[...])
    epi(acc, *rest)


def _mm(a3, w, col0, n, epi, tm, tn, extras, extra_specs, out_dtypes, name):
    bx, tx, k = a3.shape
    cb0 = col0 // tn
    out_shape = [jax.ShapeDtypeStruct((bx, tx, n), dt) for dt in out_dtypes]
    out_specs = [pl.BlockSpec((1, tm, tn), lambda b, i, j: (b, i, j)) for _ in out_dtypes]
    return pl.pallas_call(
        functools.partial(_mm_kernel, epi=epi),
        grid=(bx, tx // tm, n // tn),
        in_specs=[pl.BlockSpec((1, tm, k), lambda b, i, j: (b, i, 0)),
                  pl.BlockSpec((k, tn), lambda b, i, j: (0, cb0 + j))] + list(extra_specs),
        out_specs=out_specs,
        out_shape=out_shape,
        compiler_params=_cparams(("parallel", "parallel", "arbitrary")),
        name=name,
    )(a3, w, *extras)


def _epi_rot(acc, cos_ref, sin_ref, o_ref, *, tn):
    j = pl.program_id(2)
    scale = jnp.where(j * tn >= W_QK_R, DK_R ** -0.5, 1.0).astype(F32)
    cos = cos_ref[...]
    sin = sin_ref[...]
    for hh in range(tn // DK_R):
        xh = acc[:, hh * DK_R:(hh + 1) * DK_R]
        r = pltpu.roll(xh, DK_R // 2, axis=1)
        o_ref[0, :, hh * DK_R:(hh + 1) * DK_R] = ((xh * cos + r * sin) * scale).astype(o_ref.dtype)


def _epi_plain(acc, o_ref):
    o_ref[0] = acc.astype(o_ref.dtype)


def _epi_silu(acc, o_ref):
    o_ref[0] = (acc * jax.nn.sigmoid(acc)).astype(o_ref.dtype)


def _head_norm(acc, g_ref, gain_ref, tn):
    g = g_ref[...]
    parts = []
    for c in range(tn // MXU_DIM):
        a = acc[:, c * MXU_DIM:(c + 1) * MXU_DIM]
        hi, lo = _split_bf16(a * a)
        ms = _dot(hi, g) + _dot(lo, g)
        parts.append(a * lax.rsqrt(ms + EPS))
    y = parts[0] if len(parts) == 1 else jnp.concatenate(parts, axis=1)
    return y * gain_ref[...]


def _epi_qs(acc, g_ref, gain_ref, o_ref, *, tn):
    o_ref[0] = (_head_norm(acc, g_ref, gain_ref, tn) * HD_S ** -0.5).astype(o_ref.dtype)


def _epi_ks1(acc, g_ref, gain_ref, o_ref, *, tn):
    o_ref[0] = _head_norm(acc, g_ref, gain_ref, tn)


def _epi_gates(acc, b_ref, o_ref):
    o_ref[0] = jax.nn.sigmoid(acc + b_ref[...]).astype(o_ref.dtype)


def _mmt_kernel(a_ref, wt_ref, *rest, norm, tn, blk):
    if norm:
        g_ref, gain_ref, o_ref, ob_ref = rest
    else:
        o_ref, ob_ref = rest
    acc = lax.dot_general(wt_ref[...], a_ref[0], _NT, preferred_element_type=F32)
    if norm:
        g = g_ref[...]
        parts = []
        for c in range(tn // MXU_DIM):
            a = acc[c * MXU_DIM:(c + 1) * MXU_DIM, :]
            hi, lo = _split_bf16(a * a)
            ms = _dot(g, hi) + _dot(g, lo)
            parts.append(a * lax.rsqrt(ms + EPS))
        acc = (parts[0] if len(parts) == 1 else jnp.concatenate(parts, axis=0)) * gain_ref[...]
    o_ref[0] = acc
    for kb in range(ob_ref.shape[1]):
        ob_ref[0, kb] = acc[:, kb * blk:(kb + 1) * blk].astype(ob_ref.dtype)


def _mm_t(a3, wt, norm_inputs, tm, tn, blk, name):
    bx, tx, k = a3.shape
    n = wt.shape[0]
    norm = norm_inputs is not None
    extra, extra_specs = (), []
    if norm:
        extra = norm_inputs
        extra_specs = [pl.BlockSpec((MXU_DIM, MXU_DIM), lambda b, i, j: (0, 0)),
                       pl.BlockSpec((tn, 1), lambda b, i, j: (j, 0))]
    return pl.pallas_call(
        functools.partial(_mmt_kernel, norm=norm, tn=tn, blk=blk),
        grid=(bx, tx // tm, n // tn),
        in_specs=[pl.BlockSpec((1, tm, k), lambda b, i, j: (b, i, 0)),
                  pl.BlockSpec((tn, k), lambda b, i, j: (j, 0))] + extra_specs,
        out_specs=[pl.BlockSpec((1, tn, tm), lambda b, i, j: (b, j, i)),
                   pl.BlockSpec((1, tm // blk, tn, blk), lambda b, i, j: (b, i, j, 0))],
        out_shape=[jax.ShapeDtypeStruct((bx, n, tx), F32),
                   jax.ShapeDtypeStruct((bx, tx // blk, n, blk), BF16)],
        compiler_params=_cparams(("parallel", "parallel", "arbitrary")),
        name=name,
    )(a3, wt, *extra)


def _mixer_group(x3, mod3, cos_t, sin_t, norm_gain, w_in_b, w_mg_b, b_mg, qn_gain, kn_gain, tm, feature_major_kv):
    d = x3.shape[2]
    tn = PROJ_TN
    h = _prep(x3, mod3, norm_gain, tm)
    avg = jnp.where((jnp.arange(MXU_DIM)[:, None] // HD_S) == (jnp.arange(MXU_DIM)[None, :] // HD_S),
                    1.0 / HD_S, 0.0).astype(BF16)
    avg_spec = pl.BlockSpec((MXU_DIM, MXU_DIM), lambda b, i, j: (0, 0))
    gain_spec = pl.BlockSpec((1, tn), lambda b, i, j: (0, j))
    tab_spec = pl.BlockSpec((tm, DK_R), lambda b, i, j: (i, 0))

    c0 = 0
    (qk_r,) = _mm(h, w_in_b, c0, 2 * W_QK_R, functools.partial(_epi_rot, tn=tn), tm, tn,
                  (cos_t, sin_t), (tab_spec, tab_spec), (BF16,), "proj_qk_r")
    c0 += 2 * W_QK_R
    (v_r,) = _mm(h, w_in_b, c0, W_V_R, _epi_plain, tm, tn, (), (), (BF16,), "proj_v_r")
    c0 += W_V_R
    (sg_r,) = _mm(h, w_in_b, c0, W_V_R, _epi_silu, tm, tn, (), (), (BF16,), "proj_g_r")
    c0 += W_V_R
    (q_s,) = _mm(h, w_in_b, c0, W_S, functools.partial(_epi_qs, tn=tn), tm, tn,
                 (avg, qn_gain.reshape(1, W_S)), (avg_spec, gain_spec), (BF16,), "proj_q_s")
    c0 += W_S
    if feature_major_kv:
        blk = min(SB_BLOCK, tm)
        k_s, k_sb = _mm_t(h, w_in_b[:, c0:c0 + W_S].T, (avg, kn_gain.reshape(W_S, 1)), tm, tn, blk, "proj_k_s_t")
        c0 += W_S
        v_s, v_sb = _mm_t(h, w_in_b[:, c0:c0 + W_S].T, None, tm, tn, blk, "proj_v_s_t")
        c0 += W_S
    else:
        (k_s,) = _mm(h, w_in_b, c0, W_S, functools.partial(_epi_ks1, tn=tn), tm, tn,
                     (avg, kn_gain.reshape(1, W_S)), (avg_spec, gain_spec), (F32,), "proj_k_s")
        c0 += W_S
        (v_s,) = _mm(h, w_in_b, c0, W_S, _epi_plain, tm, tn, (), (), (F32,), "proj_v_s")
        c0 += W_S
        k_sb = v_sb = None
    (sg_s,) = _mm(h, w_in_b, c0, W_S, _epi_silu, tm, tn, (), (), (BF16,), "proj_g_s")
    (gates,) = _mm(h, w_mg_b, 0, 2 * d, _epi_gates, tm, tn,
                   (b_mg.reshape(1, 2 * d),), (gain_spec,), (BF16,), "proj_gates")
    return dict(qk_r=qk_r, v_r=v_r, sg_r=sg_r, q_s=q_s, k_s=k_s, k_sb=k_sb, v_s=v_s, v_sb=v_sb,
                sg_s=sg_s, gates=gates)


def _ret_kernel(q_ref, k_ref, v_ref, sg_ref, gn_ref, s0_ref, o_ref, st_ref, *, rows, chunk):
    @pl.when(pl.program_id(1) == 0)
    def _():
        st_ref[...] = s0_ref[...]

    ri = lax.broadcasted_iota(jnp.int32, (rows, rows), 0)
    ci = lax.broadcasted_iota(jnp.int32, (rows, rows), 1)
    diff = (ri - ci).astype(F32)
    idx = lax.broadcasted_iota(jnp.int32, (rows, 1), 0).astype(F32)
    for h in range(H_R):
        lg = math.log(1.0 - 2.0 ** (-5.0 - h))
        decay = jnp.where(diff >= 0.0, jnp.exp(lg * jnp.maximum(diff, 0.0)), 0.0)
        q_dec = jnp.exp(lg * (idx + 1.0))
        k_dec = jnp.exp(lg * (chunk - 1.0 - idx))
        q = q_ref[0, :, h * DK_R:(h + 1) * DK_R]
        k = k_ref[0, :, h * DK_R:(h + 1) * DK_R]
        v = v_ref[0, :, h * DV_R:(h + 1) * DV_R]
        st = st_ref[0, h]
        scores = lax.dot_general(q, k, _NT, preferred_element_type=F32) * decay
        inner = _dot(scores.astype(BF16), v)
        cross = _dot(q, st.astype(BF16)) * q_dec
        kd = (k.astype(F32) * k_dec).astype(BF16)
        st_ref[0, h] = math.exp(lg * chunk) * st + lax.dot_general(kd, v, _TN, preferred_element_type=F32)
        o = inner + cross
        inv = lax.rsqrt(jnp.mean(o * o, axis=-1, keepdims=True) + EPS)
        sl = slice(h * DV_R, (h + 1) * DV_R)
        o_ref[0, :, sl] = ((o * inv) * gn_ref[:, sl] * sg_ref[0, :, sl].astype(F32)).astype(o_ref.dtype)


def _retention(qk_r, v_r, sg_r, gn_r, state0, rows, chunk):
    bx, tx, _ = v_r.shape
    nc = tx // rows
    return pl.pallas_call(
        functools.partial(_ret_kernel, rows=rows, chunk=chunk),
        grid=(bx, nc),
        in_specs=[pl.BlockSpec((1, rows, W_QK_R), lambda b, c: (b, c, 0)),
                  pl.BlockSpec((1, rows, W_QK_R), lambda b, c: (b, c, 1)),
                  pl.BlockSpec((1, rows, W_V_R), lambda b, c: (b, c, 0)),
                  pl.BlockSpec((1, rows, W_V_R), lambda b, c: (b, c, 0)),
                  pl.BlockSpec((1, W_V_R), lambda b, c: (0, 0)),
                  pl.BlockSpec((1, H_R, DK_R, DV_R), lambda b, c: (b, 0, 0, 0))],
        out_specs=[pl.BlockSpec((1, rows, W_V_R), lambda b, c: (b, c, 0)),
                   pl.BlockSpec((1, H_R, DK_R, DV_R), lambda b, c: (b, 0, 0, 0))],
        out_shape=[jax.ShapeDtypeStruct((bx, tx, W_V_R), BF16),
                   jax.ShapeDtypeStruct((bx, H_R, DK_R, DV_R), F32)],
        compiler_params=_cparams(("parallel", "arbitrary")),
        name="retention",
    )(qk_r, qk_r, v_r, sg_r, gn_r.reshape(1, W_V_R), state0)


def _sb_gate(z, tri, mask):
    sp = jnp.log(1.0 + jnp.exp2(jnp.abs(z) * -LOG2E))
    log_beta = jnp.minimum(z, 0.0) - sp
    log_1mb = log_beta - z
    if mask is not None:
        log_1mb = jnp.where(mask, log_1mb, 0.0)
    c = tri.shape[0]
    parts, sums = [], []
    for ci in range(z.shape[1] // c):
        part = log_1mb[:, ci * c:(ci + 1) * c]
        parts.append(part.astype(BF16))
        sums.append(jnp.sum(part, axis=1, keepdims=True))
    return log_beta, parts, sums


def _sb_suffix(parts, tri):
    return [_dot(p, tri) for p in parts]


def _sb_weights(log_beta, local, sums, carry, mask):
    afters = [None] * len(local)
    for ci in reversed(range(len(local))):
        afters[ci] = local[ci] + carry
        carry = carry + sums[ci]
    after = afters[0] if len(afters) == 1 else jnp.concatenate(afters, axis=1)
    a = jnp.exp(log_beta + after)
    if mask is not None:
        a = jnp.where(mask, a, 0.0)
    return a.astype(BF16), carry


def _sb_values(a, vb):
    return lax.dot_general(a, vb, _NT, preferred_element_type=F32)


def _sb_step(qh, kb, vb, bias, tri, carry, mask):
    log_beta, parts, sums = _sb_gate(_dot(qh, kb) + bias, tri, mask)
    a, carry = _sb_weights(log_beta, _sb_suffix(parts, tri), sums, carry, mask)
    return carry, _sb_values(a, vb)


def _tri(n):
    return (lax.broadcasted_iota(jnp.int32, (n, n), 0) > lax.broadcasted_iota(jnp.int32, (n, n), 1)).astype(BF16)


def _sbp_kernel(bias_ref, q_ref, k_ref, v_ref, sg_ref, o_ref, qh_ref, carry_ref, acc_ref,
                z_ref, lb_ref, hl_ref, sum_ref, loc_ref, a_ref, *, blk, nh):
    hg = pl.program_id(1)
    i = pl.program_id(2)
    w2 = 2 * HD_S
    lane = lax.broadcasted_iota(jnp.int32, (1, w2), 1)
    tri = _tri(blk)
    diag = (lax.broadcasted_iota(jnp.int32, (blk, blk), 1) < lax.broadcasted_iota(jnp.int32, (blk, blk), 0))
    for h in range(nh):
        q2 = q_ref[0, :, (h // 2) * w2:(h // 2 + 1) * w2]
        qh_ref[h, :, :w2] = jnp.where((lane // HD_S) == h % 2, q2, jnp.zeros_like(q2))
        b = jnp.full((blk, w2), bias_ref[hg * nh + h], F32)
        b0 = b.astype(BF16).astype(F32)
        b1 = (b - b0).astype(BF16).astype(F32)
        b2 = b - b0 - b1
        cols = jnp.where(lane == 0, b0, jnp.where(lane == 1, b1, jnp.where(lane == 2, b2, 0.0)))
        qh_ref[h, :, w2:] = cols.astype(BF16)
    ones = jnp.ones((w2, blk), BF16)
    carry_ref[...] = jnp.zeros_like(carry_ref)
    acc_ref[...] = jnp.zeros_like(acc_ref)

    def rows(h):
        return slice((h // 2) * w2, (h // 2 + 1) * w2)

    def logits_of(h, j):
        return _dot(qh_ref[h], jnp.concatenate([k_ref[0, j, rows(h), :], ones], axis=0))

    def fold(j, mask):
        logits, gates, local, weights = {}, {}, {}, {}

        for n in range(nh + 4):
            if n < nh:
                logits[n] = logits_of(n, j)
            h = n - 2
            if 0 <= h < nh:
                local[h] = _sb_suffix(gates[h][1], tri)
            h = n - 4
            if 0 <= h < nh:
                acc_ref[h] += _sb_values(weights.pop(h), v_ref[0, j, rows(h), :])
            h = n - 1
            if 0 <= h < nh:
                gates[h] = _sb_gate(logits.pop(h), tri, mask)
            h = n - 3
            if 0 <= h < nh:
                log_beta, _, sums = gates.pop(h)
                weights[h], carry_ref[h] = _sb_weights(log_beta, local.pop(h), sums, carry_ref[h], mask)

    fold(i, diag)

    lags = dict(gate=1, suffix=2, weights=3, values=4)

    def trip(j_new, j_old, fresh, wrapped):
        def live(s, lag):
            return fresh if s >= lag else wrapped

        held = {}

        def put(ref, h, lag_p, lag_c, val):
            if (h + lag_p >= nh) == (h + lag_c >= nh):
                held[(id(ref), h)] = val
            else:
                ref[h] = val

        def get(ref, h, lag_p, lag_c):
            if (h + lag_p >= nh) == (h + lag_c >= nh):
                return held.pop((id(ref), h))
            return ref[h]

        lg, ls, lw, lv = lags["gate"], lags["suffix"], lags["weights"], lags["values"]
        for s in range(nh):
            if fresh:
                put(z_ref, s, 0, lg, logits_of(s, j_new))
            if live(s, ls):
                h = (s - ls) % nh
                put(loc_ref, h, ls, lw, _sb_suffix([get(hl_ref, h, lg, ls)], tri)[0])
            if live(s, lv):
                h = (s - lv) % nh
                jv = j_new if s >= lv else j_old
                acc_ref[h] += _sb_values(get(a_ref, h, lw, lv), v_ref[0, jv, rows(h), :])
            if live(s, lg):
                h = (s - lg) % nh
                log_beta, parts, sums = _sb_gate(get(z_ref, h, 0, lg), tri, None)
                put(lb_ref, h, lg, lw, log_beta)
                put(hl_ref, h, lg, ls, parts[0])
                put(sum_ref, h, lg, lw, sums[0])
            if live(s, lw):
                h = (s - lw) % nh
                a, carry_ref[h] = _sb_weights(get(lb_ref, h, lg, lw), [get(loc_ref, h, ls, lw)],
                                              [get(sum_ref, h, lg, lw)], carry_ref[h], None)
                put(a_ref, h, lw, lv, a)

    @pl.when(i >= 1)
    def _():
        trip(i - 1, None, True, False)

        def body(t, c):
            trip(i - 1 - t, i - t, True, True)
            return c

        lax.fori_loop(1, i, body, 0)
        trip(None, 0, False, True)

    for p in range(nh // 2):
        o = jnp.where(lane < HD_S, acc_ref[2 * p], acc_ref[2 * p + 1])
        sl = slice(p * w2, (p + 1) * w2)
        o_ref[0, :, sl] = (o * sg_ref[0, :, sl].astype(F32)).astype(o_ref.dtype)


def _sb_prompt(q_s, k_sb, v_sb, sg_s, sb_bias):
    bx, s, _ = q_s.shape
    nkb, blk = k_sb.shape[1], k_sb.shape[3]
    nh = SB_HEADS_PER_STEP
    w2 = nh * HD_S
    return pl.pallas_call(
        functools.partial(_sbp_kernel, blk=blk, nh=nh),
        grid=(bx, H_S // nh, s // blk),
        in_specs=[pl.BlockSpec(memory_space=pltpu.SMEM),
                  pl.BlockSpec((1, blk, w2), lambda b, hp, i: (b, i, hp)),
                  pl.BlockSpec((1, nkb, w2, blk), lambda b, hp, i: (b, 0, hp, 0)),
                  pl.BlockSpec((1, nkb, w2, blk), lambda b, hp, i: (b, 0, hp, 0)),
                  pl.BlockSpec((1, blk, w2), lambda b, hp, i: (b, i, hp))],
        out_specs=pl.BlockSpec((1, blk, w2), lambda b, hp, i: (b, i, hp)),
        out_shape=jax.ShapeDtypeStruct((bx, s, W_S), BF16),
        scratch_shapes=[pltpu.VMEM((nh, blk, 4 * HD_S), BF16), pltpu.VMEM((nh, blk, 1), F32),
                        pltpu.VMEM((nh, blk, 2 * HD_S), F32),
                        pltpu.VMEM((nh, blk, blk), F32), pltpu.VMEM((nh, blk, blk), F32),
                        pltpu.VMEM((nh, blk, blk), BF16), pltpu.VMEM((nh, blk, 1), F32),
                        pltpu.VMEM((nh, blk, blk), F32), pltpu.VMEM((nh, blk, blk), BF16)],
        compiler_params=_cparams(("parallel", "parallel", "arbitrary")),
        name="sb_prompt",
    )(sb_bias, q_s, k_sb, v_sb, sg_s)


def _sbs_kernel(pt_ref, bias_ref, q_ref, kn_ref, vn_ref, sg_ref, *rest, npg, page, td):
    k_refs = rest[:npg]
    v_refs = rest[npg:2 * npg]
    o_ref = rest[2 * npg]
    qbd_ref, carry_ref, acc_ref = rest[2 * npg + 1:]
    g = pl.program_id(1)
    rows = H_S * td
    tri = _tri(MXU_DIM if (npg * page) % MXU_DIM == 0 else page)
    bias = bias_ref[...]

    @pl.when(g == 0)
    def _():
        qt = jnp.concatenate([q_ref[0]] * H_S, axis=0)
        rh = lax.broadcasted_iota(jnp.int32, (rows, W_S), 0) // td
        lh = lax.broadcasted_iota(jnp.int32, (rows, W_S), 1) // HD_S
        qbd = jnp.where(rh == lh, qt, 0.0).astype(BF16)
        qbd_ref[...] = qbd
        kn = kn_ref[0].astype(BF16)
        vn = vn_ref[0].astype(BF16)
        t_of_row = lax.broadcasted_iota(jnp.int32, (rows, page), 0) % td
        mask = lax.broadcasted_iota(jnp.int32, (rows, page), 1) < t_of_row
        carry, acc = _sb_step(qbd, kn, vn, bias, _tri(page), jnp.zeros((rows, 1), F32), mask)
        carry_ref[...] = carry
        acc_ref[...] = acc

    kb = jnp.concatenate([k_refs[p][0].astype(BF16) for p in reversed(range(npg))], axis=1)
    vb = jnp.concatenate([v_refs[p][0].astype(BF16) for p in reversed(range(npg))], axis=1)
    carry, contrib = _sb_step(qbd_ref[...], kb, vb, bias, tri, carry_ref[...], None)
    acc_ref[...] += contrib
    carry_ref[...] = carry

    @pl.when(g == pl.num_programs(1) - 1)
    def _():
        acc = acc_ref[...]
        lh = lax.broadcasted_iota(jnp.int32, (td, W_S), 1) // HD_S
        out = jnp.zeros((td, W_S), F32)
        for h in range(H_S):
            out = jnp.where(lh == h, acc[h * td:(h + 1) * td, :], out)
        o_ref[0] = out * sg_ref[0]


def _sb_sample(q_s, k_new, v_new, sg_s, cache_k, cache_v, page_table, sb_bias):
    bd, td, _ = q_s.shape
    n_pages = page_table.shape[1]
    page = cache_k.shape[2]
    npg = min(PAGES_PER_STEP, n_pages)
    rows = H_S * td
    bias_col = jnp.repeat(sb_bias, td).reshape(rows, 1)

    def page_spec(p):
        return pl.BlockSpec((1, W_S, page), lambda b, g, pt: (pt[b, n_pages - 1 - (g * npg + p)], 0, 0))

    tok_spec = pl.BlockSpec((1, td, W_S), lambda b, g, pt: (b, 0, 0))
    new_spec = pl.BlockSpec((1, W_S, page), lambda b, g, pt: (b, 0, 0))
    grid_spec = pltpu.PrefetchScalarGridSpec(
        num_scalar_prefetch=1,
        grid=(bd, n_pages // npg),
        in_specs=[pl.BlockSpec((rows, 1), lambda b, g, pt: (0, 0)), tok_spec, new_spec, new_spec, tok_spec]
        + [page_spec(p) for p in range(npg)] * 2,
        out_specs=tok_spec,
        scratch_shapes=[pltpu.VMEM((rows, W_S), BF16), pltpu.VMEM((rows, 1), F32),
                        pltpu.VMEM((rows, W_S), F32)],
    )
    return pl.pallas_call(
        functools.partial(_sbs_kernel, npg=npg, page=page, td=td),
        grid_spec=grid_spec,
        out_shape=jax.ShapeDtypeStruct((bd, td, W_S), F32),
        compiler_params=_cparams(("parallel", "arbitrary"), VMEM_LIMIT_PAGED),
        name="sb_sample",
    )(page_table, bias_col, q_s, k_new, v_new, sg_s, *([cache_k] * npg), *([cache_v] * npg))


def _merge_kernel(ar_ref, wr_ref, as_ref, ws_ref, gr_ref, gs_ref, o_ref):
    y_r = _dot(ar_ref[0], wr_ref[...])
    y_s = _dot(as_ref[0], ws_ref[...])
    o_ref[0] = (gr_ref[0].astype(F32) * y_r + gs_ref[0].astype(F32) * y_s).astype(o_ref.dtype)


def _merge(br_r, br_s, gates, w_pr_b, w_ps_b, tm):
    bx, tx, _ = br_r.shape
    d = w_pr_b.shape[1]
    tn = PROJ_TN
    nb = d // tn
    return pl.pallas_call(
        _merge_kernel,
        grid=(bx, tx // tm, nb),
        in_specs=[pl.BlockSpec((1, tm, W_V_R), lambda b, i, j: (b, i, 0)),
                  pl.BlockSpec((W_V_R, tn), lambda b, i, j: (0, j)),
                  pl.BlockSpec((1, tm, W_S), lambda b, i, j: (b, i, 0)),
                  pl.BlockSpec((W_S, tn), lambda b, i, j: (0, j)),
                  pl.BlockSpec((1, tm, tn), lambda b, i, j: (b, i, j)),
                  pl.BlockSpec((1, tm, tn), lambda b, i, j: (b, i, nb + j))],
        out_specs=pl.BlockSpec((1, tm, tn), lambda b, i, j: (b, i, j)),
        out_shape=jax.ShapeDtypeStruct((bx, tx, d), BF16),
        compiler_params=_cparams(("parallel", "parallel", "arbitrary")),
        name="merge",
    )(br_r, w_pr_b, br_s, w_ps_b, gates, gates)


def _out_kernel(m_ref, w_ref, x_ref, g_ref, o_ref):
    o_ref[0] = x_ref[0] + g_ref[0] * _dot(m_ref[0], w_ref[...])


def _out_proj(merged, w_out_b, x3, mod3, tm):
    bx, tx, d = x3.shape
    tn = PROJ_TN
    return pl.pallas_call(
        _out_kernel,
        grid=(bx, tx // tm, d // tn),
        in_specs=[pl.BlockSpec((1, tm, d), lambda b, i, j: (b, i, 0)),
                  pl.BlockSpec((d, tn), lambda b, i, j: (0, j)),
                  pl.BlockSpec((1, tm, tn), lambda b, i, j: (b, i, j)),
                  _mod_spec(mod3, 2, tm, tn, d, 3)],
        out_specs=pl.BlockSpec((1, tm, tn), lambda b, i, j: (b, i, j)),
        out_shape=jax.ShapeDtypeStruct((bx, tx, d), F32),
        compiler_params=_cparams(("parallel", "parallel", "arbitrary")),
        name="out_proj",
    )(merged, w_out_b, x3, mod3)


def _rope_tables(pos):
    half = DK_R // 2
    inv_freq = ROPE_BASE ** (-jnp.arange(half, dtype=F32) / half)
    ang = pos.astype(F32)[:, None] * inv_freq[None, :]
    cos = jnp.cos(ang)
    sin = jnp.sin(ang)
    return jnp.concatenate([cos, cos], axis=1), jnp.concatenate([-sin, sin], axis=1)


def kernel(x_prompt, x_sample, cache_sb_k, cache_sb_v, state_ret, page_table, c_prompt, c_sample,
           norm_gain, w_ada, b_ada, w_in, qn_gain, kn_gain, sb_bias, gn_r, w_pr, w_ps, w_mg, b_mg, w_out):
    b, s, d = x_prompt.shape
    bd, td, _ = x_sample.shape
    n_pages = page_table.shape[1]
    page = cache_sb_k.shape[1]
    past = n_pages * page
    nd = bd * td
    assert s % SB_BLOCK == 0 or s < SB_BLOCK
    assert s % CHUNK == 0 and td <= 16 and page == LANES

    w_in_b, w_mg_b = w_in.astype(BF16), w_mg.astype(BF16)
    w_pr_b, w_ps_b, w_out_b = w_pr.astype(BF16), w_ps.astype(BF16), w_out.astype(BF16)

    mod = _ada(jnp.concatenate([c_prompt, c_sample], axis=0), w_ada, b_ada)
    mod_p = mod[:b].reshape(b, 1, 3 * d)
    mod_s = jnp.repeat(mod[b:], td, axis=0).reshape(1, nd, 3 * d)

    tm_p = min(1024, s)
    cos_p, sin_p = _rope_tables(jnp.arange(s))
    cos_s, sin_s = _rope_tables(jnp.tile(past + jnp.arange(td), bd))

    mp = _mixer_group(x_prompt, mod_p, cos_p, sin_p, norm_gain, w_in_b, w_mg_b, b_mg, qn_gain, kn_gain, tm_p, True)
    br_r, ret_state_prompt = _retention(mp["qk_r"], mp["v_r"], mp["sg_r"], gn_r,
                                        jnp.zeros((b, H_R, DK_R, DV_R), F32), CHUNK, CHUNK)
    br_s = _sb_prompt(mp["q_s"], mp["k_sb"], mp["v_sb"], mp["sg_s"], sb_bias)
    merged = _merge(br_r, br_s, mp["gates"], w_pr_b, w_ps_b, tm_p)
    y_p = _out_proj(merged, w_out_b, x_prompt, mod_p, tm_p)

    xs3 = x_sample.reshape(1, nd, d)
    ms = _mixer_group(xs3, mod_s, cos_s, sin_s, norm_gain, w_in_b, w_mg_b, b_mg, qn_gain, kn_gain, nd, False)
    rows = 16

    def pad_tok(a):
        a = a.reshape(bd, td, a.shape[-1])
        return jnp.pad(a, ((0, 0), (0, rows - td), (0, 0)))

    br_r2, ret_state_sample = _retention(pad_tok(ms["qk_r"]), pad_tok(ms["v_r"]), pad_tok(ms["sg_r"]), gn_r,
                                         state_ret, rows, td)
    br_r2 = br_r2[:, :td].reshape(1, nd, W_V_R)

    def tok_f32(a):
        return a.astype(F32).reshape(bd, td, W_S)

    def new_rows(a):
        return jnp.pad(a.reshape(bd, td, W_S).transpose(0, 2, 1), ((0, 0), (0, 0), (0, page - td)))

    def pages(c):
        return c.transpose(0, 2, 3, 1).reshape(-1, W_S, page)

    br_s2 = _sb_sample(tok_f32(ms["q_s"]), new_rows(ms["k_s"]), new_rows(ms["v_s"]), tok_f32(ms["sg_s"]),
                       pages(cache_sb_k), pages(cache_sb_v), page_table, sb_bias)
    br_s2 = br_s2.astype(BF16).reshape(1, nd, W_S)
    merged2 = _merge(br_r2, br_s2, ms["gates"], w_pr_b, w_ps_b, nd)
    y_s = _out_proj(merged2, w_out_b, xs3, mod_s, nd)

    def rows_layout(a):
        return a.reshape(b, H_S, HD_S, s).transpose(0, 3, 1, 2)

    return (y_p, y_s.reshape(bd, td, d), ret_state_prompt, ret_state_sample,
            rows_layout(mp["k_s"]), rows_layout(mp["v_s"]),
            ms["k_s"].reshape(bd, td, H_S, HD_S), ms["v_s"].reshape(bd, td, H_S, HD_S))
```

```python
import functools
import math

import jax
import jax.numpy as jnp
from jax import lax
from jax.experimental import pallas as pl
from jax.experimental.pallas import tpu as pltpu

F32 = jnp.float32
BF16 = jnp.bfloat16

H_R, DK_R, DV_R = 8, 128, 256
H_S, HD_S = 16, 64
W_QK_R = H_R * DK_R
W_V_R = H_R * DV_R
W_S = H_S * HD_S
CHUNK = 128
ROPE_BASE = 10000.0
EPS = 1e-6
LOG2E = math.log2(math.e)

LANES = 128
MXU_DIM = 256
VMEM_LIMIT = 48 * 1024 * 1024
VMEM_LIMIT_PAGED = 56 * 1024 * 1024

PROJ_TN = 1024
SB_BLOCK = 256
SB_HEADS_PER_STEP = 8
PAGES_PER_STEP = 16

_NT = (((1,), (1,)), ((), ()))
_TN = (((0,), (0,)), ((), ()))


def _cparams(sem, vmem=VMEM_LIMIT):
    return pltpu.CompilerParams(dimension_semantics=sem, vmem_limit_bytes=vmem)


def _dot(a, b):
    return jnp.dot(a, b, preferred_element_type=F32)


def _split_bf16(x):
    hi = x.astype(BF16)
    lo = (x - hi.astype(F32)).astype(BF16)
    return hi, lo


def _ada_kernel(c_ref, w_ref, b_ref, o_ref):
    @pl.when(pl.program_id(0) == 0)
    def _():
        o_ref[...] = jnp.broadcast_to(b_ref[...], o_ref.shape)

    c = c_ref[...]
    a = c * jax.nn.sigmoid(c)
    o_ref[...] += jnp.dot(a, w_ref[...], preferred_element_type=F32, precision=lax.Precision.HIGHEST)


def _ada(c_all, w_ada, b_ada):
    r, d = c_all.shape
    n = w_ada.shape[1]
    tk = MXU_DIM
    return pl.pallas_call(
        _ada_kernel,
        grid=(d // tk,),
        in_specs=[pl.BlockSpec((r, tk), lambda kk: (0, kk)),
                  pl.BlockSpec((tk, n), lambda kk: (kk, 0)),
                  pl.BlockSpec((1, n), lambda kk: (0, 0))],
        out_specs=pl.BlockSpec((r, n), lambda kk: (0, 0)),
        out_shape=jax.ShapeDtypeStruct((r, n), F32),
        compiler_params=_cparams(("arbitrary",)),
        name="ada",
    )(c_all, w_ada, b_ada.reshape(1, n))


def _prep_kernel(x_ref, g_ref, sh_ref, sc_ref, h_ref):
    x = x_ref[0]
    inv = lax.rsqrt(jnp.mean(x * x, axis=-1, keepdims=True) + EPS)
    h = (x * inv) * g_ref[...] * (1.0 + sc_ref[0]) + sh_ref[0]
    h_ref[0] = h.astype(BF16)


def _mod_spec(mod3, part, tm, tn, d, ngrid):
    per_row = mod3.shape[1] != 1
    nb = d // tn
    if ngrid == 2:
        if per_row:
            return pl.BlockSpec((1, tm, tn), lambda b, i: (b, i, part * nb))
        return pl.BlockSpec((1, 1, tn), lambda b, i: (b, 0, part * nb))
    if per_row:
        return pl.BlockSpec((1, tm, tn), lambda b, i, j: (b, i, part * nb + j))
    return pl.BlockSpec((1, 1, tn), lambda b, i, j: (b, 0, part * nb + j))


def _prep(x3, mod3, norm_gain, tm):
    bx, tx, d = x3.shape
    return pl.pallas_call(
        _prep_kernel,
        grid=(bx, tx // tm),
        in_specs=[pl.BlockSpec((1, tm, d), lambda b, i: (b, i, 0)),
                  pl.BlockSpec((1, d), lambda b, i: (0, 0)),
                  _mod_spec(mod3, 0, tm, d, d, 2),
                  _mod_spec(mod3, 1, tm, d, d, 2)],
        out_specs=pl.BlockSpec((1, tm, d), lambda b, i: (b, i, 0)),
        out_shape=jax.ShapeDtypeStruct((bx, tx, d), BF16),
        compiler_params=_cparams(("parallel", "parallel")),
        name="prep",
    )(x3, norm_gain.reshape(1, d), mod3, mod3)


def _mm_kernel(a_ref, w_ref, *rest, epi):
    wb_ref = rest[-1]

    @pl.when((pl.program_id(1) == 0) & (pl.program_id(2) == 0))
    def _():
        wb_ref[...] = w_ref[...].astype(BF16)

    epi(_dot(a_ref[0], wb_ref[...]), *rest[:-1])


def _mm(a3, w, col0, n, epi, tm, tn, extras, extra_specs, out_dtypes, name):
    bx, tx, k = a3.shape
    cb0 = col0 // tn

    def jbi(spec):
        return pl.BlockSpec(spec.block_shape, lambda j, b, i, f=spec.index_map: f(b, i, j))

    out_shape = [jax.ShapeDtypeStruct((bx, tx, n), dt) for dt in out_dtypes]
    out_specs = [pl.BlockSpec((1, tm, tn), lambda j, b, i: (b, i, j)) for _ in out_dtypes]
    return pl.pallas_call(
        functools.partial(_mm_kernel, epi=epi),
        grid=(n // tn, bx, tx // tm),
        in_specs=[pl.BlockSpec((1, tm, k), lambda j, b, i: (b, i, 0)),
                  pl.BlockSpec((k, tn), lambda j, b, i: (0, cb0 + j))] + [jbi(s) for s in extra_specs],
        out_specs=out_specs,
        out_shape=out_shape,
        scratch_shapes=[pltpu.VMEM((k, tn), BF16)],
        compiler_params=_cparams(("arbitrary", "arbitrary", "arbitrary")),
        name=name,
    )(a3, w, *extras)


def _epi_rot(acc, cos_ref, sin_ref, o_ref, *, tn):
    j = pl.program_id(0)
    scale = jnp.where(j * tn >= W_QK_R, DK_R ** -0.5, 1.0).astype(F32)
    cos = cos_ref[...]
    sin = sin_ref[...]
    for hh in range(tn // DK_R):
        xh = acc[:, hh * DK_R:(hh + 1) * DK_R]
        r = pltpu.roll(xh, DK_R // 2, axis=1)
        o_ref[0, :, hh * DK_R:(hh + 1) * DK_R] = ((xh * cos + r * sin) * scale).astype(o_ref.dtype)


def _epi_plain(acc, o_ref):
    o_ref[0] = acc.astype(o_ref.dtype)


def _epi_silu(acc, o_ref):
    o_ref[0] = (acc * jax.nn.sigmoid(acc)).astype(o_ref.dtype)


def _head_norm(acc, g_ref, gain_ref, tn):
    g = g_ref[...]
    parts = []
    for c in range(tn // MXU_DIM):
        a = acc[:, c * MXU_DIM:(c + 1) * MXU_DIM]
        hi, lo = _split_bf16(a * a)
        ms = _dot(hi, g) + _dot(lo, g)
        parts.append(a * lax.rsqrt(ms + EPS))
    y = parts[0] if len(parts) == 1 else jnp.concatenate(parts, axis=1)
    return y * gain_ref[...]


def _epi_qs(acc, g_ref, gain_ref, o_ref, *, tn):
    o_ref[0] = (_head_norm(acc, g_ref, gain_ref, tn) * HD_S ** -0.5).astype(o_ref.dtype)


def _epi_ks1(acc, g_ref, gain_ref, o_ref, *, tn):
    o_ref[0] = _head_norm(acc, g_ref, gain_ref, tn)


def _epi_gates(acc, b_ref, o_ref):
    o_ref[0] = jax.nn.sigmoid(acc + b_ref[...]).astype(o_ref.dtype)


def _mmt_kernel(a_ref, w_ref, *rest, norm, tn, blk):
    wt_ref = rest[-1]
    if norm:
        g_ref, gain_ref, o_ref, ob_ref = rest[:-1]
    else:
        o_ref, ob_ref = rest[:-1]

    @pl.when((pl.program_id(1) == 0) & (pl.program_id(2) == 0))
    def _():
        wt_ref[...] = w_ref[...].T.astype(BF16)

    acc = lax.dot_general(wt_ref[...], a_ref[0], _NT, preferred_element_type=F32)
    if norm:
        g = g_ref[...]
        parts = []
        for c in range(tn // MXU_DIM):
            a = acc[c * MXU_DIM:(c + 1) * MXU_DIM, :]
            hi, lo = _split_bf16(a * a)
            ms = _dot(g, hi) + _dot(g, lo)
            parts.append(a * lax.rsqrt(ms + EPS))
        acc = (parts[0] if len(parts) == 1 else jnp.concatenate(parts, axis=0)) * gain_ref[...]
    o_ref[0] = acc
    for kb in range(ob_ref.shape[1]):
        ob_ref[0, kb] = acc[:, kb * blk:(kb + 1) * blk].astype(ob_ref.dtype)


def _mm_t(a3, w, col0, n, norm_inputs, tm, tn, blk, name):
    bx, tx, k = a3.shape
    cb0 = col0 // tn
    norm = norm_inputs is not None
    extra, extra_specs = (), []
    if norm:
        extra = norm_inputs
        extra_specs = [pl.BlockSpec((MXU_DIM, MXU_DIM), lambda j, b, i: (0, 0)),
                       pl.BlockSpec((tn, 1), lambda j, b, i: (j, 0))]
    return pl.pallas_call(
        functools.partial(_mmt_kernel, norm=norm, tn=tn, blk=blk),
        grid=(n // tn, bx, tx // tm),
        in_specs=[pl.BlockSpec((1, tm, k), lambda j, b, i: (b, i, 0)),
                  pl.BlockSpec((k, tn), lambda j, b, i: (0, cb0 + j))] + extra_specs,
        out_specs=[pl.BlockSpec((1, tn, tm), lambda j, b, i: (b, j, i)),
                   pl.BlockSpec((1, tm // blk, tn, blk), lambda j, b, i: (b, i, j, 0))],
        out_shape=[jax.ShapeDtypeStruct((bx, n, tx), F32),
                   jax.ShapeDtypeStruct((bx, tx // blk, n, blk), BF16)],
        scratch_shapes=[pltpu.VMEM((tn, k), BF16)],
        compiler_params=_cparams(("arbitrary", "arbitrary", "arbitrary")),
        name=name,
    )(a3, w, *extra)


def _mixer_group(x3, mod3, cos_t, sin_t, norm_gain, w_in, w_mg, b_mg, qn_gain, kn_gain, tm, feature_major_kv):
    d = x3.shape[2]
    tn = PROJ_TN
    h = _prep(x3, mod3, norm_gain, tm)
    avg = jnp.where((jnp.arange(MXU_DIM)[:, None] // HD_S) == (jnp.arange(MXU_DIM)[None, :] // HD_S),
                    1.0 / HD_S, 0.0).astype(BF16)
    avg_spec = pl.BlockSpec((MXU_DIM, MXU_DIM), lambda b, i, j: (0, 0))
    gain_spec = pl.BlockSpec((1, tn), lambda b, i, j: (0, j))
    tab_spec = pl.BlockSpec((tm, DK_R), lambda b, i, j: (i, 0))

    c0 = 0
    (qk_r,) = _mm(h, w_in, c0, 2 * W_QK_R, functools.partial(_epi_rot, tn=tn), tm, tn,
                  (cos_t, sin_t), (tab_spec, tab_spec), (BF16,), "proj_qk_r")
    c0 += 2 * W_QK_R
    (v_r,) = _mm(h, w_in, c0, W_V_R, _epi_plain, tm, tn, (), (), (BF16,), "proj_v_r")
    c0 += W_V_R
    (sg_r,) = _mm(h, w_in, c0, W_V_R, _epi_silu, tm, tn, (), (), (BF16,), "proj_g_r")
    c0 += W_V_R
    (q_s,) = _mm(h, w_in, c0, W_S, functools.partial(_epi_qs, tn=tn), tm, tn,
                 (avg, qn_gain.reshape(1, W_S)), (avg_spec, gain_spec), (BF16,), "proj_q_s")
    c0 += W_S
    if feature_major_kv:
        blk = min(SB_BLOCK, tm)
        k_s, k_sb = _mm_t(h, w_in, c0, W_S, (avg, kn_gain.reshape(W_S, 1)), tm, tn, blk, "proj_k_s_t")
        c0 += W_S
        v_s, v_sb = _mm_t(h, w_in, c0, W_S, None, tm, tn, blk, "proj_v_s_t")
        c0 += W_S
    else:
        (k_s,) = _mm(h, w_in, c0, W_S, functools.partial(_epi_ks1, tn=tn), tm, tn,
                     (avg, kn_gain.reshape(1, W_S)), (avg_spec, gain_spec), (F32,), "proj_k_s")
        c0 += W_S
        (v_s,) = _mm(h, w_in, c0, W_S, _epi_plain, tm, tn, (), (), (F32,), "proj_v_s")
        c0 += W_S
        k_sb = v_sb = None
    (sg_s,) = _mm(h, w_in, c0, W_S, _epi_silu, tm, tn, (), (), (BF16,), "proj_g_s")
    (gates,) = _mm(h, w_mg, 0, 2 * d, _epi_gates, tm, tn,
                   (b_mg.reshape(1, 2 * d),), (gain_spec,), (BF16,), "proj_gates")
    return dict(qk_r=qk_r, v_r=v_r, sg_r=sg_r, q_s=q_s, k_s=k_s, k_sb=k_sb, v_s=v_s, v_sb=v_sb,
                sg_s=sg_s, gates=gates)


def _ret_kernel(q_ref, k_ref, v_ref, sg_ref, gn_ref, s0_ref, o_ref, st_ref, *, rows, chunk):
    @pl.when(pl.program_id(1) == 0)
    def _():
        st_ref[...] = s0_ref[...]

    ri = lax.broadcasted_iota(jnp.int32, (rows, rows), 0)
    ci = lax.broadcasted_iota(jnp.int32, (rows, rows), 1)
    diff = (ri - ci).astype(F32)
    idx = lax.broadcasted_iota(jnp.int32, (rows, 1), 0).astype(F32)
    for h in range(H_R):
        lg = math.log(1.0 - 2.0 ** (-5.0 - h))
        decay = jnp.where(diff >= 0.0, jnp.exp(lg * jnp.maximum(diff, 0.0)), 0.0)
        q_dec = jnp.exp(lg * (idx + 1.0))
        k_dec = jnp.exp(lg * (chunk - 1.0 - idx))
        sl = slice(h * DV_R, (h + 1) * DV_R)
        for s in range(q_ref.shape[0]):
            q = q_ref[s, :, h * DK_R:(h + 1) * DK_R]
            k = k_ref[s, :, h * DK_R:(h + 1) * DK_R]
            v = v_ref[s, :, sl]
            st = st_ref[s, h]
            scores = lax.dot_general(q, k, _NT, preferred_element_type=F32) * decay
            inner = _dot(scores.astype(BF16), v)
            cross = _dot(q, st.astype(BF16)) * q_dec
            kd = (k.astype(F32) * k_dec).astype(BF16)
            st_ref[s, h] = math.exp(lg * chunk) * st + lax.dot_general(kd, v, _TN, preferred_element_type=F32)
            o = inner + cross
            inv = lax.rsqrt(jnp.mean(o * o, axis=-1, keepdims=True) + EPS)
            o_ref[s, :, sl] = ((o * inv) * gn_ref[:, sl] * sg_ref[s, :, sl].astype(F32)).astype(o_ref.dtype)


def _retention(qk_r, v_r, sg_r, gn_r, state0, rows, chunk, bb):
    bx, tx, _ = v_r.shape
    nc = tx // rows
    return pl.pallas_call(
        functools.partial(_ret_kernel, rows=rows, chunk=chunk),
        grid=(bx // bb, nc),
        in_specs=[pl.BlockSpec((bb, rows, W_QK_R), lambda b, c: (b, c, 0)),
                  pl.BlockSpec((bb, rows, W_QK_R), lambda b, c: (b, c, 1)),
                  pl.BlockSpec((bb, rows, W_V_R), lambda b, c: (b, c, 0)),
                  pl.BlockSpec((bb, rows, W_V_R), lambda b, c: (b, c, 0)),
                  pl.BlockSpec((1, W_V_R), lambda b, c: (0, 0)),
                  pl.BlockSpec((bb, H_R, DK_R, DV_R), lambda b, c: (b, 0, 0, 0))],
        out_specs=[pl.BlockSpec((bb, rows, W_V_R), lambda b, c: (b, c, 0)),
                   pl.BlockSpec((bb, H_R, DK_R, DV_R), lambda b, c: (b, 0, 0, 0))],
        out_shape=[jax.ShapeDtypeStruct((bx, tx, W_V_R), BF16),
                   jax.ShapeDtypeStruct((bx, H_R, DK_R, DV_R), F32)],
        compiler_params=_cparams(("parallel", "arbitrary")),
        name="retention",
    )(qk_r, qk_r, v_r, sg_r, gn_r.reshape(1, W_V_R), state0)


def _sb_gate(z, tri, mask):
    sp = jnp.log(1.0 + jnp.exp2(jnp.abs(z) * -LOG2E))
    log_beta = jnp.minimum(z, 0.0) - sp
    log_1mb = log_beta - z
    if mask is not None:
        log_1mb = jnp.where(mask, log_1mb, 0.0)
    c = tri.shape[0]
    parts, sums = [], []
    for ci in range(z.shape[1] // c):
        part = log_1mb[:, ci * c:(ci + 1) * c]
        parts.append(part.astype(BF16))
        sums.append(jnp.sum(part, axis=1, keepdims=True))
    return log_beta, parts, sums


def _sb_suffix(parts, tri):
    return [_dot(p, tri) for p in parts]


def _sb_weights(log_beta, local, sums, carry, mask):
    afters = [None] * len(local)
    for ci in reversed(range(len(local))):
        afters[ci] = local[ci] + carry
        carry = carry + sums[ci]
    after = afters[0] if len(afters) == 1 else jnp.concatenate(afters, axis=1)
    a = jnp.exp(log_beta + after)
    if mask is not None:
        a = jnp.where(mask, a, 0.0)
    return a.astype(BF16), carry


def _sb_values(a, vb):
    return lax.dot_general(a, vb, _NT, preferred_element_type=F32)


def _sb_step(qh, kb, vb, bias, tri, carry, mask):
    log_beta, parts, sums = _sb_gate(_dot(qh, kb) + bias, tri, mask)
    a, carry = _sb_weights(log_beta, _sb_suffix(parts, tri), sums, carry, mask)
    return carry, _sb_values(a, vb)


def _tri(n):
    return (lax.broadcasted_iota(jnp.int32, (n, n), 0) > lax.broadcasted_iota(jnp.int32, (n, n), 1)).astype(BF16)


def _sbp_kernel(bias_ref, q_ref, k_ref, v_ref, sg_ref, o_ref, qh_ref, carry_ref, acc_ref,
                z_ref, lb_ref, hl_ref, sum_ref, loc_ref, a_ref, *, blk, nh):
    hg = pl.program_id(1)
    i = pl.program_id(2)
    w2 = 2 * HD_S
    lane = lax.broadcasted_iota(jnp.int32, (1, w2), 1)
    tri = _tri(blk)
    diag = (lax.broadcasted_iota(jnp.int32, (blk, blk), 1) < lax.broadcasted_iota(jnp.int32, (blk, blk), 0))
    for h in range(nh):
        q2 = q_ref[0, :, (h // 2) * w2:(h // 2 + 1) * w2]
        qh_ref[h, :, :w2] = jnp.where((lane // HD_S) == h % 2, q2, jnp.zeros_like(q2))
        b = jnp.full((blk, w2), bias_ref[hg * nh + h], F32)
        b0 = b.astype(BF16).astype(F32)
        b1 = (b - b0).astype(BF16).astype(F32)
        b2 = b - b0 - b1
        cols = jnp.where(lane == 0, b0, jnp.where(lane == 1, b1, jnp.where(lane == 2, b2, 0.0)))
        qh_ref[h, :, w2:] = cols.astype(BF16)
    ones = jnp.ones((w2, blk), BF16)
    carry_ref[...] = jnp.zeros_like(carry_ref)
    acc_ref[...] = jnp.zeros_like(acc_ref)

    def rows(h):
        return slice((h // 2) * w2, (h // 2 + 1) * w2)

    def logits_of(h, j):
        return _dot(qh_ref[h], jnp.concatenate([k_ref[0, j, rows(h), :], ones], axis=0))

    def fold(j, mask):
        logits, gates, local, weights = {}, {}, {}, {}

        for n in range(nh + 4):
            if n < nh:
                logits[n] = logits_of(n, j)
            h = n - 2
            if 0 <= h < nh:
                local[h] = _sb_suffix(gates[h][1], tri)
            h = n - 4
            if 0 <= h < nh:
                acc_ref[h] += _sb_values(weights.pop(h), v_ref[0, j, rows(h), :])
            h = n - 1
            if 0 <= h < nh:
                gates[h] = _sb_gate(logits.pop(h), tri, mask)
            h = n - 3
            if 0 <= h < nh:
                log_beta, _, sums = gates.pop(h)
                weights[h], carry_ref[h] = _sb_weights(log_beta, local.pop(h), sums, carry_ref[h], mask)

    fold(i, diag)

    lags = dict(gate=1, suffix=2, weights=3, values=4)

    def trip(j_new, j_old, fresh, wrapped):
        def live(s, lag):
            return fresh if s >= lag else wrapped

        held = {}

        def put(ref, h, lag_p, lag_c, val):
            if (h + lag_p >= nh) == (h + lag_c >= nh):
                held[(id(ref), h)] = val
            else:
                ref[h] = val

        def get(ref, h, lag_p, lag_c):
            if (h + lag_p >= nh) == (h + lag_c >= nh):
                return held.pop((id(ref), h))
            return ref[h]

        lg, ls, lw, lv = lags["gate"], lags["suffix"], lags["weights"], lags["values"]
        for s in range(nh):
            if fresh:
                put(z_ref, s, 0, lg, logits_of(s, j_new))
            if live(s, ls):
                h = (s - ls) % nh
                put(loc_ref, h, ls, lw, _sb_suffix([get(hl_ref, h, lg, ls)], tri)[0])
            if live(s, lv):
                h = (s - lv) % nh
                jv = j_new if s >= lv else j_old
                acc_ref[h] += _sb_values(get(a_ref, h, lw, lv), v_ref[0, jv, rows(h), :])
            if live(s, lg):
                h = (s - lg) % nh
                log_beta, parts, sums = _sb_gate(get(z_ref, h, 0, lg), tri, None)
                put(lb_ref, h, lg, lw, log_beta)
                put(hl_ref, h, lg, ls, parts[0])
                put(sum_ref, h, lg, lw, sums[0])
            if live(s, lw):
                h = (s - lw) % nh
                a, carry_ref[h] = _sb_weights(get(lb_ref, h, lg, lw), [get(loc_ref, h, ls, lw)],
                                              [get(sum_ref, h, lg, lw)], carry_ref[h], None)
                put(a_ref, h, lw, lv, a)

    @pl.when(i >= 1)
    def _():
        trip(i - 1, None, True, False)

        def body(t, c):
            trip(i - 1 - t, i - t, True, True)
            return c

        lax.fori_loop(1, i, body, 0)
        trip(None, 0, False, True)

    for p in range(nh // 2):
        o = jnp.where(lane < HD_S, acc_ref[2 * p], acc_ref[2 * p + 1])
        sl = slice(p * w2, (p + 1) * w2)
        o_ref[0, :, sl] = (o * sg_ref[0, :, sl].astype(F32)).astype(o_ref.dtype)


def _sb_prompt(q_s, k_sb, v_sb, sg_s, sb_bias):
    bx, s, _ = q_s.shape
    nkb, blk = k_sb.shape[1], k_sb.shape[3]
    nh = SB_HEADS_PER_STEP
    w2 = nh * HD_S
    return pl.pallas_call(
        functools.partial(_sbp_kernel, blk=blk, nh=nh),
        grid=(bx, H_S // nh, s // blk),
        in_specs=[pl.BlockSpec(memory_space=pltpu.SMEM),
                  pl.BlockSpec((1, blk, w2), lambda b, hp, i: (b, i, hp)),
                  pl.BlockSpec((1, nkb, w2, blk), lambda b, hp, i: (b, 0, hp, 0)),
                  pl.BlockSpec((1, nkb, w2, blk), lambda b, hp, i: (b, 0, hp, 0)),
                  pl.BlockSpec((1, blk, w2), lambda b, hp, i: (b, i, hp))],
        out_specs=pl.BlockSpec((1, blk, w2), lambda b, hp, i: (b, i, hp)),
        out_shape=jax.ShapeDtypeStruct((bx, s, W_S), BF16),
        scratch_shapes=[pltpu.VMEM((nh, blk, 4 * HD_S), BF16), pltpu.VMEM((nh, blk, 1), F32),
                        pltpu.VMEM((nh, blk, 2 * HD_S), F32),
                        pltpu.VMEM((nh, blk, blk), F32), pltpu.VMEM((nh, blk, blk), F32),
                        pltpu.VMEM((nh, blk, blk), BF16), pltpu.VMEM((nh, blk, 1), F32),
                        pltpu.VMEM((nh, blk, blk), F32), pltpu.VMEM((nh, blk, blk), BF16)],
        compiler_params=_cparams(("parallel", "parallel", "arbitrary")),
        name="sb_prompt",
    )(sb_bias, q_s, k_sb, v_sb, sg_s)


def _sbs_kernel(pt_ref, bias_ref, q_ref, kn_ref, vn_ref, sg_ref, *rest, npg, page, td):
    k_refs = rest[:npg]
    v_refs = rest[npg:2 * npg]
    o_ref = rest[2 * npg]
    qbd_ref, carry_ref, acc_ref = rest[2 * npg + 1:]
    g = pl.program_id(1)
    rows = H_S * td
    tri = _tri(MXU_DIM if (npg * page) % MXU_DIM == 0 else page)
    bias = bias_ref[...]

    @pl.when(g == 0)
    def _():
        qt = jnp.concatenate([q_ref[0]] * H_S, axis=0)
        rh = lax.broadcasted_iota(jnp.int32, (rows, W_S), 0) // td
        lh = lax.broadcasted_iota(jnp.int32, (rows, W_S), 1) // HD_S
        qbd = jnp.where(rh == lh, qt, 0.0).astype(BF16)
        qbd_ref[...] = qbd
        pad = jnp.zeros((page - td, W_S), F32)
        kn = jnp.concatenate([kn_ref[0], pad], axis=0).astype(BF16)
        vn = jnp.concatenate([vn_ref[0], pad], axis=0).astype(BF16)
        t_of_row = lax.broadcasted_iota(jnp.int32, (rows, page), 0) % td
        mask = lax.broadcasted_iota(jnp.int32, (rows, page), 1) < t_of_row
        z = lax.dot_general(qbd, kn, _NT, preferred_element_type=F32) + bias
        log_beta, parts, sums = _sb_gate(z, _tri(page), mask)
        a, carry = _sb_weights(log_beta, _sb_suffix(parts, _tri(page)), sums, jnp.zeros((rows, 1), F32), mask)
        carry_ref[...] = carry
        acc_ref[...] = _dot(a, vn)

    kb = jnp.concatenate([k_refs[p][0].astype(BF16) for p in reversed(range(npg))], axis=1)
    vb = jnp.concatenate([v_refs[p][0].astype(BF16) for p in reversed(range(npg))], axis=1)
    carry, contrib = _sb_step(qbd_ref[...], kb, vb, bias, tri, carry_ref[...], None)
    acc_ref[...] += contrib
    carry_ref[...] = carry

    @pl.when(g == pl.num_programs(1) - 1)
    def _():
        acc = acc_ref[...]
        lh = lax.broadcasted_iota(jnp.int32, (td, W_S), 1) // HD_S
        out = jnp.zeros((td, W_S), F32)
        for h in range(H_S):
            out = jnp.where(lh == h, acc[h * td:(h + 1) * td, :], out)
        o_ref[0] = out * sg_ref[0]


def _sb_sample(q_s, k_new, v_new, sg_s, cache_k, cache_v, page_table, sb_bias):
    bd, td, _ = q_s.shape
    n_pages = page_table.shape[1]
    page = cache_k.shape[2]
    npg = min(PAGES_PER_STEP, n_pages)
    rows = H_S * td
    bias_col = jnp.repeat(sb_bias, td).reshape(rows, 1)

    def page_spec(p):
        return pl.BlockSpec((1, W_S, page), lambda b, g, pt: (pt[b, n_pages - 1 - (g * npg + p)], 0, 0))

    tok_spec = pl.BlockSpec((1, td, W_S), lambda b, g, pt: (b, 0, 0))
    grid_spec = pltpu.PrefetchScalarGridSpec(
        num_scalar_prefetch=1,
        grid=(bd, n_pages // npg),
        in_specs=[pl.BlockSpec((rows, 1), lambda b, g, pt: (0, 0)), tok_spec, tok_spec, tok_spec, tok_spec]
        + [page_spec(p) for p in range(npg)] * 2,
        out_specs=tok_spec,
        scratch_shapes=[pltpu.VMEM((rows, W_S), BF16), pltpu.VMEM((rows, 1), F32),
                        pltpu.VMEM((rows, W_S), F32)],
    )
    return pl.pallas_call(
        functools.partial(_sbs_kernel, npg=npg, page=page, td=td),
        grid_spec=grid_spec,
        out_shape=jax.ShapeDtypeStruct((bd, td, W_S), F32),
        compiler_params=_cparams(("parallel", "arbitrary"), VMEM_LIMIT_PAGED),
        name="sb_sample",
    )(page_table, bias_col, q_s, k_new, v_new, sg_s, *([cache_k] * npg), *([cache_v] * npg))


def _merge_kernel(ar_ref, wr_ref, as_ref, ws_ref, gr_ref, gs_ref, o_ref):
    y_r = _dot(ar_ref[0], wr_ref[...])
    y_s = _dot(as_ref[0], ws_ref[...])
    o_ref[0] = (gr_ref[0].astype(F32) * y_r + gs_ref[0].astype(F32) * y_s).astype(o_ref.dtype)


def _merge(br_r, br_s, gates, w_pr_b, w_ps_b, tm):
    bx, tx, _ = br_r.shape
    d = w_pr_b.shape[1]
    tn = PROJ_TN
    nb = d // tn
    return pl.pallas_call(
        _merge_kernel,
        grid=(bx, tx // tm, nb),
        in_specs=[pl.BlockSpec((1, tm, W_V_R), lambda b, i, j: (b, i, 0)),
                  pl.BlockSpec((W_V_R, tn), lambda b, i, j: (0, j)),
                  pl.BlockSpec((1, tm, W_S), lambda b, i, j: (b, i, 0)),
                  pl.BlockSpec((W_S, tn), lambda b, i, j: (0, j)),
                  pl.BlockSpec((1, tm, tn), lambda b, i, j: (b, i, j)),
                  pl.BlockSpec((1, tm, tn), lambda b, i, j: (b, i, nb + j))],
        out_specs=pl.BlockSpec((1, tm, tn), lambda b, i, j: (b, i, j)),
        out_shape=jax.ShapeDtypeStruct((bx, tx, d), BF16),
        compiler_params=_cparams(("parallel", "parallel", "arbitrary")),
        name="merge",
    )(br_r, w_pr_b, br_s, w_ps_b, gates, gates)


def _out_kernel(m_ref, w_ref, x_ref, g_ref, o_ref):
    o_ref[0] = x_ref[0] + g_ref[0] * _dot(m_ref[0], w_ref[...])


def _out_proj(merged, w_out_b, x3, mod3, tm):
    bx, tx, d = x3.shape
    tn = PROJ_TN
    return pl.pallas_call(
        _out_kernel,
        grid=(bx, tx // tm, d // tn),
        in_specs=[pl.BlockSpec((1, tm, d), lambda b, i, j: (b, i, 0)),
                  pl.BlockSpec((d, tn), lambda b, i, j: (0, j)),
                  pl.BlockSpec((1, tm, tn), lambda b, i, j: (b, i, j)),
                  _mod_spec(mod3, 2, tm, tn, d, 3)],
        out_specs=pl.BlockSpec((1, tm, tn), lambda b, i, j: (b, i, j)),
        out_shape=jax.ShapeDtypeStruct((bx, tx, d), F32),
        compiler_params=_cparams(("parallel", "parallel", "arbitrary")),
        name="out_proj",
    )(merged, w_out_b, x3, mod3)


def _rope_tables(pos):
    half = DK_R // 2
    inv_freq = ROPE_BASE ** (-jnp.arange(half, dtype=F32) / half)
    ang = pos.astype(F32)[:, None] * inv_freq[None, :]
    cos = jnp.cos(ang)
    sin = jnp.sin(ang)
    return jnp.concatenate([cos, cos], axis=1), jnp.concatenate([-sin, sin], axis=1)


def kernel(x_prompt, x_sample, cache_sb_k, cache_sb_v, state_ret, page_table, c_prompt, c_sample,
           norm_gain, w_ada, b_ada, w_in, qn_gain, kn_gain, sb_bias, gn_r, w_pr, w_ps, w_mg, b_mg, w_out):
    b, s, d = x_prompt.shape
    bd, td, _ = x_sample.shape
    n_pages = page_table.shape[1]
    page = cache_sb_k.shape[1]
    past = n_pages * page
    nd = bd * td
    assert s % SB_BLOCK == 0 or s < SB_BLOCK
    assert s % CHUNK == 0 and td <= 16 and page == LANES

    w_pr_b, w_ps_b, w_out_b = w_pr.astype(BF16), w_ps.astype(BF16), w_out.astype(BF16)

    mod = _ada(jnp.concatenate([c_prompt, c_sample], axis=0), w_ada, b_ada)
    mod_p = mod[:b].reshape(b, 1, 3 * d)
    mod_s = jnp.repeat(mod[b:], td, axis=0).reshape(1, nd, 3 * d)

    tm_p = min(1024, s)
    cos_p, sin_p = _rope_tables(jnp.arange(s))
    cos_s, sin_s = _rope_tables(jnp.tile(past + jnp.arange(td), bd))

    mp = _mixer_group(x_prompt, mod_p, cos_p, sin_p, norm_gain, w_in, w_mg, b_mg, qn_gain, kn_gain, tm_p, True)
    br_r, ret_state_prompt = _retention(mp["qk_r"], mp["v_r"], mp["sg_r"], gn_r,
                                        jnp.zeros((b, H_R, DK_R, DV_R), F32), CHUNK, CHUNK, b)
    br_s = _sb_prompt(mp["q_s"], mp["k_sb"], mp["v_sb"], mp["sg_s"], sb_bias)
    merged = _merge(br_r, br_s, mp["gates"], w_pr_b, w_ps_b, tm_p)
    y_p = _out_proj(merged, w_out_b, x_prompt, mod_p, tm_p)

    xs3 = x_sample.reshape(1, nd, d)
    ms = _mixer_group(xs3, mod_s, cos_s, sin_s, norm_gain, w_in, w_mg, b_mg, qn_gain, kn_gain, nd, False)
    rows = 16

    def pad_tok(a):
        a = a.reshape(bd, td, a.shape[-1])
        return jnp.pad(a, ((0, 0), (0, rows - td), (0, 0)))

    br_r2, ret_state_sample = _retention(pad_tok(ms["qk_r"]), pad_tok(ms["v_r"]), pad_tok(ms["sg_r"]), gn_r,
                                         state_ret, rows, td, math.gcd(bd, 4))
    br_r2 = br_r2[:, :td].reshape(1, nd, W_V_R)

    def tok_f32(a):
        return a.astype(F32).reshape(bd, td, W_S)

    def pages(c):
        return c.transpose(0, 2, 3, 1).reshape(-1, W_S, page)

    br_s2 = _sb_sample(tok_f32(ms["q_s"]), tok_f32(ms["k_s"]), tok_f32(ms["v_s"]), tok_f32(ms["sg_s"]),
                       pages(cache_sb_k), pages(cache_sb_v), page_table, sb_bias)
    br_s2 = br_s2.astype(BF16).reshape(1, nd, W_S)
    merged2 = _merge(br_r2, br_s2, ms["gates"], w_pr_b, w_ps_b, nd)
    y_s = _out_proj(merged2, w_out_b, xs3, mod_s, nd)

    def rows_layout(a):
        return a.reshape(b, H_S, HD_S, s).transpose(0, 3, 1, 2)

    return (y_p, y_s.reshape(bd, td, d), ret_state_prompt, ret_state_sample,
            rows_layout(mp["k_s"]), rows_layout(mp["v_s"]),
            ms["k_s"].reshape(bd, td, H_S, HD_S), ms["v_s"].reshape(bd, td, H_S, HD_S))
```

```python
import functools
import math

import jax
import jax.numpy as jnp
from jax import lax
from jax.experimental import pallas as pl
from jax.experimental.pallas import tpu as pltpu

F32 = jnp.float32
BF16 = jnp.bfloat16

H_R, DK_R, DV_R = 8, 128, 256
H_S, HD_S = 16, 64
W_QK_R = H_R * DK_R
W_V_R = H_R * DV_R
W_S = H_S * HD_S
CHUNK = 128
ROPE_BASE = 10000.0
EPS = 1e-6
LOG2E = math.log2(math.e)

LANES = 128
MXU_DIM = 256
VMEM_LIMIT = 48 * 1024 * 1024
VMEM_LIMIT_PAGED = 56 * 1024 * 1024

PROJ_TN = 1024
SB_BLOCK = 256
SB_HEADS_PER_STEP = 8
PAGES_PER_STEP = 16

_NT = (((1,), (1,)), ((), ()))
_TN = (((0,), (0,)), ((), ()))


def _cparams(sem, vmem=VMEM_LIMIT):
    return pltpu.CompilerParams(dimension_semantics=sem, vmem_limit_bytes=vmem)


def _dot(a, b):
    return jnp.dot(a, b, preferred_element_type=F32)


def _split_bf16(x):
    hi = x.astype(BF16)
    lo = (x - hi.astype(F32)).astype(BF16)
    return hi, lo


def _ada_kernel(c_ref, w_ref, b_ref, o_ref):
    @pl.when(pl.program_id(0) == 0)
    def _():
        o_ref[...] = jnp.broadcast_to(b_ref[...], o_ref.shape)

    c = c_ref[...]
    a = c * jax.nn.sigmoid(c)
    o_ref[...] += jnp.dot(a, w_ref[...], preferred_element_type=F32, precision=lax.Precision.HIGHEST)


def _ada(c_all, w_ada, b_ada):
    r, d = c_all.shape
    n = w_ada.shape[1]
    tk = MXU_DIM
    return pl.pallas_call(
        _ada_kernel,
        grid=(d // tk,),
        in_specs=[pl.BlockSpec((r, tk), lambda kk: (0, kk)),
                  pl.BlockSpec((tk, n), lambda kk: (kk, 0)),
                  pl.BlockSpec((1, n), lambda kk: (0, 0))],
        out_specs=pl.BlockSpec((r, n), lambda kk: (0, 0)),
        out_shape=jax.ShapeDtypeStruct((r, n), F32),
        compiler_params=_cparams(("arbitrary",)),
        name="ada",
    )(c_all, w_ada, b_ada.reshape(1, n))


def _prep_kernel(x_ref, g_ref, sh_ref, sc_ref, h_ref):
    x = x_ref[0]
    inv = lax.rsqrt(jnp.mean(x * x, axis=-1, keepdims=True) + EPS)
    h = (x * inv) * g_ref[...] * (1.0 + sc_ref[0]) + sh_ref[0]
    h_ref[0] = h.astype(BF16)


def _mod_spec(mod3, part, tm, tn, d, ngrid):
    per_row = mod3.shape[1] != 1
    nb = d // tn
    if ngrid == 2:
        if per_row:
            return pl.BlockSpec((1, tm, tn), lambda b, i: (b, i, part * nb))
        return pl.BlockSpec((1, 1, tn), lambda b, i: (b, 0, part * nb))
    if per_row:
        return pl.BlockSpec((1, tm, tn), lambda b, i, j: (b, i, part * nb + j))
    return pl.BlockSpec((1, 1, tn), lambda b, i, j: (b, 0, part * nb + j))


def _prep(x3, mod3, norm_gain, tm):
    bx, tx, d = x3.shape
    return pl.pallas_call(
        _prep_kernel,
        grid=(bx, tx // tm),
        in_specs=[pl.BlockSpec((1, tm, d), lambda b, i: (b, i, 0)),
                  pl.BlockSpec((1, d), lambda b, i: (0, 0)),
                  _mod_spec(mod3, 0, tm, d, d, 2),
                  _mod_spec(mod3, 1, tm, d, d, 2)],
        out_specs=pl.BlockSpec((1, tm, d), lambda b, i: (b, i, 0)),
        out_shape=jax.ShapeDtypeStruct((bx, tx, d), BF16),
        compiler_params=_cparams(("parallel", "parallel")),
        name="prep",
    )(x3, norm_gain.reshape(1, d), mod3, mod3)


def _split_side_refs(rest, ne, no, nse, nso):
    main_extras, pos = rest[:ne], ne
    side_in = ()
    if nso:
        side_in, pos = rest[pos:pos + 1 + nse], pos + 1 + nse
    main_outs, pos = rest[pos:pos + no], pos + no
    return main_extras, side_in, main_outs, rest[pos:pos + nso], rest[-1]


def _mm_kernel(a_ref, w_ref, *rest, epi, ne, no, side_epi, nse, nso):
    main_extras, side_in, main_outs, side_outs, wb_ref = _split_side_refs(rest, ne, no, nse, nso)

    @pl.when((pl.program_id(1) == 0) & (pl.program_id(2) == 0))
    def _():
        wb_ref[...] = w_ref[...].astype(BF16)
        if nso:
            side_epi(_dot(side_in[0][0], wb_ref[...]), *side_in[1:], *side_outs)

    epi(_dot(a_ref[0], wb_ref[...]), *main_extras, *main_outs)


def _side_specs(side, k, tn, jbi):
    if side is None:
        return [], [], (), []
    ms = side["a"].shape[1]
    in_specs = [pl.BlockSpec((1, ms, k), lambda j, b, i: (0, 0, 0))] + [jbi(s) for s in side["specs"]]
    out_specs = [pl.BlockSpec((1, ms, tn), lambda j, b, i: (0, 0, j)) for _ in side["out_dtypes"]]
    return in_specs, out_specs, (side["a"],) + tuple(side["extras"]), [
        (1, ms, dt) for dt in side["out_dtypes"]]


def _mm(a3, w, col0, n, epi, tm, tn, extras, extra_specs, out_dtypes, name, side=None):
    bx, tx, k = a3.shape
    cb0 = col0 // tn

    def jbi(spec):
        return pl.BlockSpec(spec.block_shape, lambda j, b, i, f=spec.index_map: f(b, i, j))

    s_in, s_out, s_ops, s_shapes = _side_specs(side, k, tn, jbi)
    out_shape = [jax.ShapeDtypeStruct((bx, tx, n), dt) for dt in out_dtypes]
    out_shape += [jax.ShapeDtypeStruct((sb, sr, n), dt) for sb, sr, dt in s_shapes]
    out_specs = [pl.BlockSpec((1, tm, tn), lambda j, b, i: (b, i, j)) for _ in out_dtypes] + s_out
    return pl.pallas_call(
        functools.partial(_mm_kernel, epi=epi, ne=len(extras), no=len(out_dtypes),
                          side_epi=side["epi"] if side else None, nse=len(s_ops) - 1 if side else 0,
                          nso=len(s_out)),
        grid=(n // tn, bx, tx // tm),
        in_specs=[pl.BlockSpec((1, tm, k), lambda j, b, i: (b, i, 0)),
                  pl.BlockSpec((k, tn), lambda j, b, i: (0, cb0 + j))] + [jbi(s) for s in extra_specs] + s_in,
        out_specs=out_specs,
        out_shape=out_shape,
        scratch_shapes=[pltpu.VMEM((k, tn), BF16)],
        compiler_params=_cparams(("arbitrary", "arbitrary", "arbitrary")),
        name=name,
    )(a3, w, *extras, *s_ops)


def _epi_rot(acc, cos_ref, sin_ref, o_ref, *, tn):
    j = pl.program_id(0)
    scale = jnp.where(j * tn >= W_QK_R, DK_R ** -0.5, 1.0).astype(F32)
    cos = cos_ref[...]
    sin = sin_ref[...]
    for hh in range(tn // DK_R):
        xh = acc[:, hh * DK_R:(hh + 1) * DK_R]
        r = pltpu.roll(xh, DK_R // 2, axis=1)
        o_ref[0, :, hh * DK_R:(hh + 1) * DK_R] = ((xh * cos + r * sin) * scale).astype(o_ref.dtype)


def _epi_plain(acc, o_ref):
    o_ref[0] = acc.astype(o_ref.dtype)


def _epi_silu(acc, o_ref):
    o_ref[0] = (acc * jax.nn.sigmoid(acc)).astype(o_ref.dtype)


def _head_norm(acc, g_ref, gain_ref, tn):
    g = g_ref[...]
    parts = []
    for c in range(tn // MXU_DIM):
        a = acc[:, c * MXU_DIM:(c + 1) * MXU_DIM]
        hi, lo = _split_bf16(a * a)
        ms = _dot(hi, g) + _dot(lo, g)
        parts.append(a * lax.rsqrt(ms + EPS))
    y = parts[0] if len(parts) == 1 else jnp.concatenate(parts, axis=1)
    return y * gain_ref[...]


def _epi_qs(acc, g_ref, gain_ref, o_ref, *, tn):
    o_ref[0] = (_head_norm(acc, g_ref, gain_ref, tn) * HD_S ** -0.5).astype(o_ref.dtype)


def _epi_ks1(acc, g_ref, gain_ref, o_ref, *, tn):
    o_ref[0] = _head_norm(acc, g_ref, gain_ref, tn)


def _epi_gates(acc, b_ref, o_ref):
    o_ref[0] = jax.nn.sigmoid(acc + b_ref[...]).astype(o_ref.dtype)


def _mmt_kernel(a_ref, w_ref, *rest, norm, tn, blk, side_epi, nse):
    main_extras, side_in, main_outs, side_outs, wt_ref = _split_side_refs(rest, 2 if norm else 0, 2, nse, 1)
    if norm:
        g_ref, gain_ref = main_extras
    o_ref, ob_ref = main_outs

    @pl.when((pl.program_id(1) == 0) & (pl.program_id(2) == 0))
    def _():
        wt_ref[...] = w_ref[...].T.astype(BF16)
        side_epi(lax.dot_general(side_in[0][0], wt_ref[...], _NT, preferred_element_type=F32),
                 *side_in[1:], *side_outs)

    acc = lax.dot_general(wt_ref[...], a_ref[0], _NT, preferred_element_type=F32)
    if norm:
        g = g_ref[...]
        parts = []
        for c in range(tn // MXU_DIM):
            a = acc[c * MXU_DIM:(c + 1) * MXU_DIM, :]
            hi, lo = _split_bf16(a * a)
            ms = _dot(g, hi) + _dot(g, lo)
            parts.append(a * lax.rsqrt(ms + EPS))
        acc = (parts[0] if len(parts) == 1 else jnp.concatenate(parts, axis=0)) * gain_ref[...]
    o_ref[0] = acc
    for kb in range(ob_ref.shape[1]):
        ob_ref[0, kb] = acc[:, kb * blk:(kb + 1) * blk].astype(ob_ref.dtype)


def _mm_t(a3, w, col0, n, norm_inputs, tm, tn, blk, name, side):
    bx, tx, k = a3.shape
    cb0 = col0 // tn
    norm = norm_inputs is not None
    extra, extra_specs = (), []
    if norm:
        extra = norm_inputs
        extra_specs = [pl.BlockSpec((MXU_DIM, MXU_DIM), lambda j, b, i: (0, 0)),
                       pl.BlockSpec((tn, 1), lambda j, b, i: (j, 0))]

    def jbi(spec):
        return pl.BlockSpec(spec.block_shape, lambda j, b, i, f=spec.index_map: f(b, i, j))

    s_in, s_out, s_ops, s_shapes = _side_specs(side, k, tn, jbi)
    return pl.pallas_call(
        functools.partial(_mmt_kernel, norm=norm, tn=tn, blk=blk, side_epi=side["epi"], nse=len(s_ops) - 1),
        grid=(n // tn, bx, tx // tm),
        in_specs=[pl.BlockSpec((1, tm, k), lambda j, b, i: (b, i, 0)),
                  pl.BlockSpec((k, tn), lambda j, b, i: (0, cb0 + j))] + extra_specs + s_in,
        out_specs=[pl.BlockSpec((1, tn, tm), lambda j, b, i: (b, j, i)),
                   pl.BlockSpec((1, tm // blk, tn, blk), lambda j, b, i: (b, i, j, 0))] + s_out,
        out_shape=[jax.ShapeDtypeStruct((bx, n, tx), F32),
                   jax.ShapeDtypeStruct((bx, tx // blk, n, blk), BF16)]
        + [jax.ShapeDtypeStruct((sb, sr, n), dt) for sb, sr, dt in s_shapes],
        scratch_shapes=[pltpu.VMEM((tn, k), BF16)],
        compiler_params=_cparams(("arbitrary", "arbitrary", "arbitrary")),
        name=name,
    )(a3, w, *extra, *s_ops)


def _mixer_groups(xp3, modp3, tabs_p, xs3, mods3, tabs_s, norm_gain, w_in, w_mg, b_mg, qn_gain, kn_gain, tm):
    d = xp3.shape[2]
    nd = xs3.shape[1]
    tn = PROJ_TN
    hp = _prep(xp3, modp3, norm_gain, tm)
    hs = _prep(xs3, mods3, norm_gain, nd)
    avg = jnp.where((jnp.arange(MXU_DIM)[:, None] // HD_S) == (jnp.arange(MXU_DIM)[None, :] // HD_S),
                    1.0 / HD_S, 0.0).astype(BF16)
    avg_spec = pl.BlockSpec((MXU_DIM, MXU_DIM), lambda b, i, j: (0, 0))
    gain_spec = pl.BlockSpec((1, tn), lambda b, i, j: (0, j))
    tab_spec = pl.BlockSpec((tm, DK_R), lambda b, i, j: (i, 0))
    tab_spec_s = pl.BlockSpec((nd, DK_R), lambda b, i, j: (0, 0))

    def both(w, col0, n, epi, extras, specs, dts, name, s_extras=None, s_specs=None):
        side = dict(a=hs, extras=extras if s_extras is None else s_extras,
                    specs=specs if s_specs is None else s_specs, out_dtypes=dts, epi=epi)
        outs = _mm(hp, w, col0, n, epi, tm, tn, extras, specs, dts, name, side)
        return outs[0], outs[1]

    mp, ms = {}, {}
    c0 = 0
    mp["qk_r"], ms["qk_r"] = both(w_in, c0, 2 * W_QK_R, functools.partial(_epi_rot, tn=tn), tabs_p,
                                  (tab_spec, tab_spec), (BF16,), "proj_qk_r", tabs_s, (tab_spec_s, tab_spec_s))
    c0 += 2 * W_QK_R
    mp["v_r"], ms["v_r"] = both(w_in, c0, W_V_R, _epi_plain, (), (), (BF16,), "proj_v_r")
    c0 += W_V_R
    mp["sg_r"], ms["sg_r"] = both(w_in, c0, W_V_R, _epi_silu, (), (), (BF16,), "proj_g_r")
    c0 += W_V_R
    mp["q_s"], ms["q_s"] = both(w_in, c0, W_S, functools.partial(_epi_qs, tn=tn),
                                (avg, qn_gain.reshape(1, W_S)), (avg_spec, gain_spec), (BF16,), "proj_q_s")
    c0 += W_S
    blk = min(SB_BLOCK, tm)
    side_k = dict(a=hs, extras=(avg, kn_gain.reshape(1, W_S)), specs=(avg_spec, gain_spec), out_dtypes=(F32,),
                  epi=functools.partial(_epi_ks1, tn=tn))
    mp["k_s"], mp["k_sb"], ms["k_s"] = _mm_t(hp, w_in, c0, W_S, (avg, kn_gain.reshape(W_S, 1)), tm, tn, blk,
                                            "proj_k_s_t", side_k)
    c0 += W_S
    side_v = dict(a=hs, extras=(), specs=(), out_dtypes=(F32,), epi=_epi_plain)
    mp["v_s"], mp["v_sb"], ms["v_s"] = _mm_t(hp, w_in, c0, W_S, None, tm, tn, blk, "proj_v_s_t", side_v)
    c0 += W_S
    mp["sg_s"], ms["sg_s"] = both(w_in, c0, W_S, _epi_silu, (), (), (BF16,), "proj_g_s")
    mp["gates"], ms["gates"] = both(w_mg, 0, 2 * d, _epi_gates, (b_mg.reshape(1, 2 * d),), (gain_spec,), (BF16,),
                                    "proj_gates")
    return mp, ms


def _ret_kernel(q_ref, k_ref, v_ref, sg_ref, gn_ref, s0_ref, o_ref, st_ref, *, rows, chunk):
    @pl.when(pl.program_id(1) == 0)
    def _():
        st_ref[...] = s0_ref[...]

    ri = lax.broadcasted_iota(jnp.int32, (rows, rows), 0)
    ci = lax.broadcasted_iota(jnp.int32, (rows, rows), 1)
    diff = (ri - ci).astype(F32)
    idx = lax.broadcasted_iota(jnp.int32, (rows, 1), 0).astype(F32)
    for h in range(H_R):
        lg = math.log(1.0 - 2.0 ** (-5.0 - h))
        decay = jnp.where(diff >= 0.0, jnp.exp(lg * jnp.maximum(diff, 0.0)), 0.0)
        q_dec = jnp.exp(lg * (idx + 1.0))
        k_dec = jnp.exp(lg * (chunk - 1.0 - idx))
        sl = slice(h * DV_R, (h + 1) * DV_R)
        for s in range(q_ref.shape[0]):
            q = q_ref[s, :, h * DK_R:(h + 1) * DK_R]
            k = k_ref[s, :, h * DK_R:(h + 1) * DK_R]
            v = v_ref[s, :, sl]
            st = st_ref[s, h]
            scores = lax.dot_general(q, k, _NT, preferred_element_type=F32) * decay
            inner = _dot(scores.astype(BF16), v)
            cross = _dot(q, st.astype(BF16)) * q_dec
            kd = (k.astype(F32) * k_dec).astype(BF16)
            st_ref[s, h] = math.exp(lg * chunk) * st + lax.dot_general(kd, v, _TN, preferred_element_type=F32)
            o = inner + cross
            inv = lax.rsqrt(jnp.mean(o * o, axis=-1, keepdims=True) + EPS)
            o_ref[s, :, sl] = ((o * inv) * gn_ref[:, sl] * sg_ref[s, :, sl].astype(F32)).astype(o_ref.dtype)


def _retention(qk_r, v_r, sg_r, gn_r, state0, rows, chunk, bb):
    bx, tx, _ = v_r.shape
    nc = tx // rows
    return pl.pallas_call(
        functools.partial(_ret_kernel, rows=rows, chunk=chunk),
        grid=(bx // bb, nc),
        in_specs=[pl.BlockSpec((bb, rows, W_QK_R), lambda b, c: (b, c, 0)),
                  pl.BlockSpec((bb, rows, W_QK_R), lambda b, c: (b, c, 1)),
                  pl.BlockSpec((bb, rows, W_V_R), lambda b, c: (b, c, 0)),
                  pl.BlockSpec((bb, rows, W_V_R), lambda b, c: (b, c, 0)),
                  pl.BlockSpec((1, W_V_R), lambda b, c: (0, 0)),
                  pl.BlockSpec((bb, H_R, DK_R, DV_R), lambda b, c: (b, 0, 0, 0))],
        out_specs=[pl.BlockSpec((bb, rows, W_V_R), lambda b, c: (b, c, 0)),
                   pl.BlockSpec((bb, H_R, DK_R, DV_R), lambda b, c: (b, 0, 0, 0))],
        out_shape=[jax.ShapeDtypeStruct((bx, tx, W_V_R), BF16),
                   jax.ShapeDtypeStruct((bx, H_R, DK_R, DV_R), F32)],
        compiler_params=_cparams(("parallel", "arbitrary")),
        name="retention",
    )(qk_r, qk_r, v_r, sg_r, gn_r.reshape(1, W_V_R), state0)


def _sb_gate(z, tri, mask):
    sp = jnp.log(1.0 + jnp.exp2(jnp.abs(z) * -LOG2E))
    log_beta = jnp.minimum(z, 0.0) - sp
    log_1mb = log_beta - z
    if mask is not None:
        log_1mb = jnp.where(mask, log_1mb, 0.0)
    c = tri.shape[0]
    parts, sums = [], []
    for ci in range(z.shape[1] // c):
        part = log_1mb[:, ci * c:(ci + 1) * c]
        parts.append(part.astype(BF16))
        sums.append(jnp.sum(part, axis=1, keepdims=True))
    return log_beta, parts, sums


def _sb_suffix(parts, tri):
    return [_dot(p, tri) for p in parts]


def _sb_weights(log_beta, local, sums, carry, mask):
    afters = [None] * len(local)
    for ci in reversed(range(len(local))):
        afters[ci] = local[ci] + carry
        carry = carry + sums[ci]
    after = afters[0] if len(afters) == 1 else jnp.concatenate(afters, axis=1)
    a = jnp.exp(log_beta + after)
    if mask is not None:
        a = jnp.where(mask, a, 0.0)
    return a.astype(BF16), carry


def _sb_values(a, vb):
    return lax.dot_general(a, vb, _NT, preferred_element_type=F32)


def _sb_step(qh, kb, vb, bias, tri, carry, mask):
    log_beta, parts, sums = _sb_gate(_dot(qh, kb) + bias, tri, mask)
    a, carry = _sb_weights(log_beta, _sb_suffix(parts, tri), sums, carry, mask)
    return carry, _sb_values(a, vb)


def _tri(n):
    return (lax.broadcasted_iota(jnp.int32, (n, n), 0) > lax.broadcasted_iota(jnp.int32, (n, n), 1)).astype(BF16)


def _sbp_kernel(bias_ref, q_ref, k_ref, v_ref, sg_ref, o_ref, qh_ref, carry_ref, acc_ref,
                z_ref, lb_ref, hl_ref, sum_ref, loc_ref, a_ref, *, blk, nh):
    hg = pl.program_id(1)
    i = pl.program_id(2)
    w2 = 2 * HD_S
    lane = lax.broadcasted_iota(jnp.int32, (1, w2), 1)
    tri = _tri(blk)
    diag = (lax.broadcasted_iota(jnp.int32, (blk, blk), 1) < lax.broadcasted_iota(jnp.int32, (blk, blk), 0))
    for h in range(nh):
        q2 = q_ref[0, :, (h // 2) * w2:(h // 2 + 1) * w2]
        qh_ref[h, :, :w2] = jnp.where((lane // HD_S) == h % 2, q2, jnp.zeros_like(q2))
        b = jnp.full((blk, w2), bias_ref[hg * nh + h], F32)
        b0 = b.astype(BF16).astype(F32)
        b1 = (b - b0).astype(BF16).astype(F32)
        b2 = b - b0 - b1
        cols = jnp.where(lane == 0, b0, jnp.where(lane == 1, b1, jnp.where(lane == 2, b2, 0.0)))
        qh_ref[h, :, w2:] = cols.astype(BF16)
    ones = jnp.ones((w2, blk), BF16)
    carry_ref[...] = jnp.zeros_like(carry_ref)
    acc_ref[...] = jnp.zeros_like(acc_ref)

    def rows(h):
        return slice((h // 2) * w2, (h // 2 + 1) * w2)

    def logits_of(h, j):
        return _dot(qh_ref[h], jnp.concatenate([k_ref[0, j, rows(h), :], ones], axis=0))

    def fold(j, mask):
        logits, gates, local, weights = {}, {}, {}, {}

        for n in range(nh + 4):
            if n < nh:
                logits[n] = logits_of(n, j)
            h = n - 2
            if 0 <= h < nh:
                local[h] = _sb_suffix(gates[h][1], tri)
            h = n - 4
            if 0 <= h < nh:
                acc_ref[h] += _sb_values(weights.pop(h), v_ref[0, j, rows(h), :])
            h = n - 1
            if 0 <= h < nh:
                gates[h] = _sb_gate(logits.pop(h), tri, mask)
            h = n - 3
            if 0 <= h < nh:
                log_beta, _, sums = gates.pop(h)
                weights[h], carry_ref[h] = _sb_weights(log_beta, local.pop(h), sums, carry_ref[h], mask)

    fold(i, diag)

    lags = dict(gate=1, suffix=2, weights=3, values=4)

    def trip(j_new, j_old, fresh, wrapped):
        def live(s, lag):
            return fresh if s >= lag else wrapped

        held = {}

        def put(ref, h, lag_p, lag_c, val):
            if (h + lag_p >= nh) == (h + lag_c >= nh):
                held[(id(ref), h)] = val
            else:
                ref[h] = val

        def get(ref, h, lag_p, lag_c):
            if (h + lag_p >= nh) == (h + lag_c >= nh):
                return held.pop((id(ref), h))
            return ref[h]

        lg, ls, lw, lv = lags["gate"], lags["suffix"], lags["weights"], lags["values"]
        for s in range(nh):
            if fresh:
                put(z_ref, s, 0, lg, logits_of(s, j_new))
            if live(s, ls):
                h = (s - ls) % nh
                put(loc_ref, h, ls, lw, _sb_suffix([get(hl_ref, h, lg, ls)], tri)[0])
            if live(s, lv):
                h = (s - lv) % nh
                jv = j_new if s >= lv else j_old
                acc_ref[h] += _sb_values(get(a_ref, h, lw, lv), v_ref[0, jv, rows(h), :])
            if live(s, lg):
                h = (s - lg) % nh
                log_beta, parts, sums = _sb_gate(get(z_ref, h, 0, lg), tri, None)
                put(lb_ref, h, lg, lw, log_beta)
                put(hl_ref, h, lg, ls, parts[0])
                put(sum_ref, h, lg, lw, sums[0])
            if live(s, lw):
                h = (s - lw) % nh
                a, carry_ref[h] = _sb_weights(get(lb_ref, h, lg, lw), [get(loc_ref, h, ls, lw)],
                                              [get(sum_ref, h, lg, lw)], carry_ref[h], None)
                put(a_ref, h, lw, lv, a)

    @pl.when(i >= 1)
    def _():
        trip(i - 1, None, True, False)

        def body(t, c):
            trip(i - 1 - t, i - t, True, True)
            return c

        lax.fori_loop(1, i, body, 0)
        trip(None, 0, False, True)

    for p in range(nh // 2):
        o = jnp.where(lane < HD_S, acc_ref[2 * p], acc_ref[2 * p + 1])
        sl = slice(p * w2, (p + 1) * w2)
        o_ref[0, :, sl] = (o * sg_ref[0, :, sl].astype(F32)).astype(o_ref.dtype)


def _sb_prompt(q_s, k_sb, v_sb, sg_s, sb_bias):
    bx, s, _ = q_s.shape
    nkb, blk = k_sb.shape[1], k_sb.shape[3]
    nh = SB_HEADS_PER_STEP
    w2 = nh * HD_S
    return pl.pallas_call(
        functools.partial(_sbp_kernel, blk=blk, nh=nh),
        grid=(bx, H_S // nh, s // blk),
        in_specs=[pl.BlockSpec(memory_space=pltpu.SMEM),
                  pl.BlockSpec((1, blk, w2), lambda b, hp, i: (b, i, hp)),
                  pl.BlockSpec((1, nkb, w2, blk), lambda b, hp, i: (b, 0, hp, 0)),
                  pl.BlockSpec((1, nkb, w2, blk), lambda b, hp, i: (b, 0, hp, 0)),
                  pl.BlockSpec((1, blk, w2), lambda b, hp, i: (b, i, hp))],
        out_specs=pl.BlockSpec((1, blk, w2), lambda b, hp, i: (b, i, hp)),
        out_shape=jax.ShapeDtypeStruct((bx, s, W_S), BF16),
        scratch_shapes=[pltpu.VMEM((nh, blk, 4 * HD_S), BF16), pltpu.VMEM((nh, blk, 1), F32),
                        pltpu.VMEM((nh, blk, 2 * HD_S), F32),
                        pltpu.VMEM((nh, blk, blk), F32), pltpu.VMEM((nh, blk, blk), F32),
                        pltpu.VMEM((nh, blk, blk), BF16), pltpu.VMEM((nh, blk, 1), F32),
                        pltpu.VMEM((nh, blk, blk), F32), pltpu.VMEM((nh, blk, blk), BF16)],
        compiler_params=_cparams(("parallel", "parallel", "arbitrary")),
        name="sb_prompt",
    )(sb_bias, q_s, k_sb, v_sb, sg_s)


def _sbs_kernel(pt_ref, bias_ref, q_ref, kn_ref, vn_ref, sg_ref, *rest, npg, page, td):
    k_refs = rest[:npg]
    v_refs = rest[npg:2 * npg]
    o_ref = rest[2 * npg]
    qbd_ref, carry_ref, acc_ref = rest[2 * npg + 1:]
    g = pl.program_id(1)
    rows = H_S * td
    tri = _tri(MXU_DIM if (npg * page) % MXU_DIM == 0 else page)
    bias = bias_ref[...]

    @pl.when(g == 0)
    def _():
        qt = jnp.concatenate([q_ref[0]] * H_S, axis=0)
        rh = lax.broadcasted_iota(jnp.int32, (rows, W_S), 0) // td
        lh = lax.broadcasted_iota(jnp.int32, (rows, W_S), 1) // HD_S
        qbd = jnp.where(rh == lh, qt, 0.0).astype(BF16)
        qbd_ref[...] = qbd
        pad = jnp.zeros((page - td, W_S), F32)
        kn = jnp.concatenate([kn_ref[0], pad], axis=0).astype(BF16)
        vn = jnp.concatenate([vn_ref[0], pad], axis=0).astype(BF16)
        t_of_row = lax.broadcasted_iota(jnp.int32, (rows, page), 0) % td
        mask = lax.broadcasted_iota(jnp.int32, (rows, page), 1) < t_of_row
        z = lax.dot_general(qbd, kn, _NT, preferred_element_type=F32) + bias
        log_beta, parts, sums = _sb_gate(z, _tri(page), mask)
        a, carry = _sb_weights(log_beta, _sb_suffix(parts, _tri(page)), sums, jnp.zeros((rows, 1), F32), mask)
        carry_ref[...] = carry
        acc_ref[...] = _dot(a, vn)

    kb = jnp.concatenate([k_refs[p][0].astype(BF16) for p in reversed(range(npg))], axis=1)
    vb = jnp.concatenate([v_refs[p][0].astype(BF16) for p in reversed(range(npg))], axis=1)
    carry, contrib = _sb_step(qbd_ref[...], kb, vb, bias, tri, carry_ref[...], None)
    acc_ref[...] += contrib
    carry_ref[...] = carry

    @pl.when(g == pl.num_programs(1) - 1)
    def _():
        acc = acc_ref[...]
        lh = lax.broadcasted_iota(jnp.int32, (td, W_S), 1) // HD_S
        out = jnp.zeros((td, W_S), F32)
        for h in range(H_S):
            out = jnp.where(lh == h, acc[h * td:(h + 1) * td, :], out)
        o_ref[0] = out * sg_ref[0]


def _sb_sample(q_s, k_new, v_new, sg_s, cache_k, cache_v, page_table, sb_bias):
    bd, td, _ = q_s.shape
    n_pages = page_table.shape[1]
    page = cache_k.shape[2]
    npg = min(PAGES_PER_STEP, n_pages)
    rows = H_S * td
    bias_col = jnp.repeat(sb_bias, td).reshape(rows, 1)

    def page_spec(p):
        return pl.BlockSpec((1, W_S, page), lambda b, g, pt: (pt[b, n_pages - 1 - (g * npg + p)], 0, 0))

    tok_spec = pl.BlockSpec((1, td, W_S), lambda b, g, pt: (b, 0, 0))
    grid_spec = pltpu.PrefetchScalarGridSpec(
        num_scalar_prefetch=1,
        grid=(bd, n_pages // npg),
        in_specs=[pl.BlockSpec((rows, 1), lambda b, g, pt: (0, 0)), tok_spec, tok_spec, tok_spec, tok_spec]
        + [page_spec(p) for p in range(npg)] * 2,
        out_specs=tok_spec,
        scratch_shapes=[pltpu.VMEM((rows, W_S), BF16), pltpu.VMEM((rows, 1), F32),
                        pltpu.VMEM((rows, W_S), F32)],
    )
    return pl.pallas_call(
        functools.partial(_sbs_kernel, npg=npg, page=page, td=td),
        grid_spec=grid_spec,
        out_shape=jax.ShapeDtypeStruct((bd, td, W_S), F32),
        compiler_params=_cparams(("parallel", "arbitrary"), VMEM_LIMIT_PAGED),
        name="sb_sample",
    )(page_table, bias_col, q_s, k_new, v_new, sg_s, *([cache_k] * npg), *([cache_v] * npg))


def _merge_kernel(ar_ref, wr_ref, as_ref, ws_ref, gr_ref, gs_ref, o_ref):
    y_r = _dot(ar_ref[0], wr_ref[...])
    y_s = _dot(as_ref[0], ws_ref[...])
    o_ref[0] = (gr_ref[0].astype(F32) * y_r + gs_ref[0].astype(F32) * y_s).astype(o_ref.dtype)


def _merge(br_r, br_s, gates, w_pr_b, w_ps_b, tm):
    bx, tx, _ = br_r.shape
    d = w_pr_b.shape[1]
    tn = PROJ_TN
    nb = d // tn
    return pl.pallas_call(
        _merge_kernel,
        grid=(bx, tx // tm, nb),
        in_specs=[pl.BlockSpec((1, tm, W_V_R), lambda b, i, j: (b, i, 0)),
                  pl.BlockSpec((W_V_R, tn), lambda b, i, j: (0, j)),
                  pl.BlockSpec((1, tm, W_S), lambda b, i, j: (b, i, 0)),
                  pl.BlockSpec((W_S, tn), lambda b, i, j: (0, j)),
                  pl.BlockSpec((1, tm, tn), lambda b, i, j: (b, i, j)),
                  pl.BlockSpec((1, tm, tn), lambda b, i, j: (b, i, nb + j))],
        out_specs=pl.BlockSpec((1, tm, tn), lambda b, i, j: (b, i, j)),
        out_shape=jax.ShapeDtypeStruct((bx, tx, d), BF16),
        compiler_params=_cparams(("parallel", "parallel", "arbitrary")),
        name="merge",
    )(br_r, w_pr_b, br_s, w_ps_b, gates, gates)


def _out_kernel(m_ref, w_ref, x_ref, g_ref, o_ref):
    o_ref[0] = x_ref[0] + g_ref[0] * _dot(m_ref[0], w_ref[...])


def _out_proj(merged, w_out_b, x3, mod3, tm):
    bx, tx, d = x3.shape
    tn = PROJ_TN
    return pl.pallas_call(
        _out_kernel,
        grid=(bx, tx // tm, d // tn),
        in_specs=[pl.BlockSpec((1, tm, d), lambda b, i, j: (b, i, 0)),
                  pl.BlockSpec((d, tn), lambda b, i, j: (0, j)),
                  pl.BlockSpec((1, tm, tn), lambda b, i, j: (b, i, j)),
                  _mod_spec(mod3, 2, tm, tn, d, 3)],
        out_specs=pl.BlockSpec((1, tm, tn), lambda b, i, j: (b, i, j)),
        out_shape=jax.ShapeDtypeStruct((bx, tx, d), F32),
        compiler_params=_cparams(("parallel", "parallel", "arbitrary")),
        name="out_proj",
    )(merged, w_out_b, x3, mod3)


def _rope_tables(pos):
    half = DK_R // 2
    inv_freq = ROPE_BASE ** (-jnp.arange(half, dtype=F32) / half)
    ang = pos.astype(F32)[:, None] * inv_freq[None, :]
    cos = jnp.cos(ang)
    sin = jnp.sin(ang)
    return jnp.concatenate([cos, cos], axis=1), jnp.concatenate([-sin, sin], axis=1)


def kernel(x_prompt, x_sample, cache_sb_k, cache_sb_v, state_ret, page_table, c_prompt, c_sample,
           norm_gain, w_ada, b_ada, w_in, qn_gain, kn_gain, sb_bias, gn_r, w_pr, w_ps, w_mg, b_mg, w_out):
    b, s, d = x_prompt.shape
    bd, td, _ = x_sample.shape
    n_pages = page_table.shape[1]
    page = cache_sb_k.shape[1]
    past = n_pages * page
    nd = bd * td
    assert s % SB_BLOCK == 0 or s < SB_BLOCK
    assert s % CHUNK == 0 and td <= 16 and page == LANES

    w_pr_b, w_ps_b, w_out_b = w_pr.astype(BF16), w_ps.astype(BF16), w_out.astype(BF16)

    mod = _ada(jnp.concatenate([c_prompt, c_sample], axis=0), w_ada, b_ada)
    mod_p = mod[:b].reshape(b, 1, 3 * d)
    mod_s = jnp.repeat(mod[b:], td, axis=0).reshape(1, nd, 3 * d)

    tm_p = min(1024, s)
    cos_p, sin_p = _rope_tables(jnp.arange(s))
    cos_s, sin_s = _rope_tables(jnp.tile(past + jnp.arange(td), bd))

    xs3 = x_sample.reshape(1, nd, d)
    mp, ms = _mixer_groups(x_prompt, mod_p, (cos_p, sin_p), xs3, mod_s, (cos_s, sin_s), norm_gain, w_in, w_mg,
                           b_mg, qn_gain, kn_gain, tm_p)

    br_r, ret_state_prompt = _retention(mp["qk_r"], mp["v_r"], mp["sg_r"], gn_r,
                                        jnp.zeros((b, H_R, DK_R, DV_R), F32), CHUNK, CHUNK, b)
    br_s = _sb_prompt(mp["q_s"], mp["k_sb"], mp["v_sb"], mp["sg_s"], sb_bias)
    merged = _merge(br_r, br_s, mp["gates"], w_pr_b, w_ps_b, tm_p)
    y_p = _out_proj(merged, w_out_b, x_prompt, mod_p, tm_p)

    rows = 16

    def pad_tok(a):
        a = a.reshape(bd, td, a.shape[-1])
        return jnp.pad(a, ((0, 0), (0, rows - td), (0, 0)))

    br_r2, ret_state_sample = _retention(pad_tok(ms["qk_r"]), pad_tok(ms["v_r"]), pad_tok(ms["sg_r"]), gn_r,
                                         state_ret, rows, td, math.gcd(bd, 4))
    br_r2 = br_r2[:, :td].reshape(1, nd, W_V_R)

    def tok_f32(a):
        return a.astype(F32).reshape(bd, td, W_S)

    def pages(c):
        return c.transpose(0, 2, 3, 1).reshape(-1, W_S, page)

    br_s2 = _sb_sample(tok_f32(ms["q_s"]), tok_f32(ms["k_s"]), tok_f32(ms["v_s"]), tok_f32(ms["sg_s"]),
                       pages(cache_sb_k), pages(cache_sb_v), page_table, sb_bias)
    br_s2 = br_s2.astype(BF16).reshape(1, nd, W_S)
    merged2 = _merge(br_r2, br_s2, ms["gates"], w_pr_b, w_ps_b, nd)
    y_s = _out_proj(merged2, w_out_b, xs3, mod_s, nd)

    def rows_layout(a):
        return a.reshape(b, H_S, HD_S, s).transpose(0, 3, 1, 2)

    return (y_p, y_s.reshape(bd, td, d), ret_state_prompt, ret_state_sample,
            rows_layout(mp["k_s"]), rows_layout(mp["v_s"]),
            ms["k_s"].reshape(bd, td, H_S, HD_S), ms["v_s"].reshape(bd, td, H_S, HD_S))
```

```python
import functools
import math

import jax
import jax.numpy as jnp
from jax import lax
from jax.experimental import pallas as pl
from jax.experimental.pallas import tpu as pltpu

F32 = jnp.float32
BF16 = jnp.bfloat16

H_R, DK_R, DV_R = 8, 128, 256
H_S, HD_S = 16, 64
W_QK_R = H_R * DK_R
W_V_R = H_R * DV_R
W_S = H_S * HD_S
CHUNK = 128
ROPE_BASE = 10000.0
EPS = 1e-6
LOG2E = math.log2(math.e)

LANES = 128
MXU_DIM = 256
VMEM_LIMIT = 48 * 1024 * 1024
VMEM_LIMIT_PAGED = 56 * 1024 * 1024

PROJ_TN = 1024
SB_BLOCK = 256
SB_HEADS_PER_STEP = 8
PAGES_PER_STEP = 16

_NT = (((1,), (1,)), ((), ()))
_TN = (((0,), (0,)), ((), ()))


def _cparams(sem, vmem=VMEM_LIMIT):
    return pltpu.CompilerParams(dimension_semantics=sem, vmem_limit_bytes=vmem)


def _dot(a, b):
    return jnp.dot(a, b, preferred_element_type=F32)


def _split_bf16(x):
    hi = x.astype(BF16)
    lo = (x - hi.astype(F32)).astype(BF16)
    return hi, lo


def _ada_kernel(c_ref, w_ref, b_ref, o_ref):
    @pl.when(pl.program_id(0) == 0)
    def _():
        o_ref[...] = jnp.broadcast_to(b_ref[...], o_ref.shape)

    c = c_ref[...]
    a = c * jax.nn.sigmoid(c)
    o_ref[...] += jnp.dot(a, w_ref[...], preferred_element_type=F32, precision=lax.Precision.HIGHEST)


def _ada(c_all, w_ada, b_ada):
    r, d = c_all.shape
    n = w_ada.shape[1]
    tk = MXU_DIM
    return pl.pallas_call(
        _ada_kernel,
        grid=(d // tk,),
        in_specs=[pl.BlockSpec((r, tk), lambda kk: (0, kk)),
                  pl.BlockSpec((tk, n), lambda kk: (kk, 0)),
                  pl.BlockSpec((1, n), lambda kk: (0, 0))],
        out_specs=pl.BlockSpec((r, n), lambda kk: (0, 0)),
        out_shape=jax.ShapeDtypeStruct((r, n), F32),
        compiler_params=_cparams(("arbitrary",)),
        name="ada",
    )(c_all, w_ada, b_ada.reshape(1, n))


def _prep_kernel(x_ref, g_ref, sh_ref, sc_ref, h_ref):
    x = x_ref[0]
    inv = lax.rsqrt(jnp.mean(x * x, axis=-1, keepdims=True) + EPS)
    h = (x * inv) * g_ref[...] * (1.0 + sc_ref[0]) + sh_ref[0]
    h_ref[0] = h.astype(BF16)


def _mod_spec(mod3, part, tm, tn, d, ngrid):
    per_row = mod3.shape[1] != 1
    nb = d // tn
    if ngrid == 2:
        if per_row:
            return pl.BlockSpec((1, tm, tn), lambda b, i: (b, i, part * nb))
        return pl.BlockSpec((1, 1, tn), lambda b, i: (b, 0, part * nb))
    if per_row:
        return pl.BlockSpec((1, tm, tn), lambda b, i, j: (b, i, part * nb + j))
    return pl.BlockSpec((1, 1, tn), lambda b, i, j: (b, 0, part * nb + j))


def _prep(x3, mod3, norm_gain, tm):
    bx, tx, d = x3.shape
    return pl.pallas_call(
        _prep_kernel,
        grid=(bx, tx // tm),
        in_specs=[pl.BlockSpec((1, tm, d), lambda b, i: (b, i, 0)),
                  pl.BlockSpec((1, d), lambda b, i: (0, 0)),
                  _mod_spec(mod3, 0, tm, d, d, 2),
                  _mod_spec(mod3, 1, tm, d, d, 2)],
        out_specs=pl.BlockSpec((1, tm, d), lambda b, i: (b, i, 0)),
        out_shape=jax.ShapeDtypeStruct((bx, tx, d), BF16),
        compiler_params=_cparams(("parallel", "parallel")),
        name="prep",
    )(x3, norm_gain.reshape(1, d), mod3, mod3)


def _split_side_refs(rest, ne, no, nse, nso):
    main_extras, pos = rest[:ne], ne
    side_in = ()
    if nso:
        side_in, pos = rest[pos:pos + 1 + nse], pos + 1 + nse
    main_outs, pos = rest[pos:pos + no], pos + no
    return main_extras, side_in, main_outs, rest[pos:pos + nso], rest[-1]


def _mm_kernel(a_ref, w_ref, *rest, epi, ne, no, side_epi, nse, nso):
    main_extras, side_in, main_outs, side_outs, wb_ref = _split_side_refs(rest, ne, no, nse, nso)

    @pl.when((pl.program_id(1) == 0) & (pl.program_id(2) == 0))
    def _():
        wb_ref[...] = w_ref[...].astype(BF16)
        if nso:
            side_epi(_dot(side_in[0][0], wb_ref[...]), *side_in[1:], *side_outs)

    epi(_dot(a_ref[0], wb_ref[...]), *main_extras, *main_outs)


def _side_specs(side, k, tn, jbi):
    if side is None:
        return [], [], (), []
    ms = side["a"].shape[1]
    in_specs = [pl.BlockSpec((1, ms, k), lambda j, b, i: (0, 0, 0))] + [jbi(s) for s in side["specs"]]
    out_specs = [pl.BlockSpec((1, ms, tn), lambda j, b, i: (0, 0, j)) for _ in side["out_dtypes"]]
    return in_specs, out_specs, (side["a"],) + tuple(side["extras"]), [
        (1, ms, dt) for dt in side["out_dtypes"]]


def _mm(a3, w, col0, n, epi, tm, tn, extras, extra_specs, out_dtypes, name, side=None):
    bx, tx, k = a3.shape
    cb0 = col0 // tn

    def jbi(spec):
        return pl.BlockSpec(spec.block_shape, lambda j, b, i, f=spec.index_map: f(b, i, j))

    s_in, s_out, s_ops, s_shapes = _side_specs(side, k, tn, jbi)
    out_shape = [jax.ShapeDtypeStruct((bx, tx, n), dt) for dt in out_dtypes]
    out_shape += [jax.ShapeDtypeStruct((sb, sr, n), dt) for sb, sr, dt in s_shapes]
    out_specs = [pl.BlockSpec((1, tm, tn), lambda j, b, i: (b, i, j)) for _ in out_dtypes] + s_out
    return pl.pallas_call(
        functools.partial(_mm_kernel, epi=epi, ne=len(extras), no=len(out_dtypes),
                          side_epi=side["epi"] if side else None, nse=len(s_ops) - 1 if side else 0,
                          nso=len(s_out)),
        grid=(n // tn, bx, tx // tm),
        in_specs=[pl.BlockSpec((1, tm, k), lambda j, b, i: (b, i, 0)),
                  pl.BlockSpec((k, tn), lambda j, b, i: (0, cb0 + j))] + [jbi(s) for s in extra_specs] + s_in,
        out_specs=out_specs,
        out_shape=out_shape,
        scratch_shapes=[pltpu.VMEM((k, tn), BF16)],
        compiler_params=_cparams(("arbitrary", "arbitrary", "arbitrary")),
        name=name,
    )(a3, w, *extras, *s_ops)


def _epi_rot(acc, cos_ref, sin_ref, o_ref, *, tn):
    j = pl.program_id(0)
    scale = jnp.where(j * tn >= W_QK_R, DK_R ** -0.5, 1.0).astype(F32)
    cos = cos_ref[...]
    sin = sin_ref[...]
    for hh in range(tn // DK_R):
        xh = acc[:, hh * DK_R:(hh + 1) * DK_R]
        r = pltpu.roll(xh, DK_R // 2, axis=1)
        o_ref[0, :, hh * DK_R:(hh + 1) * DK_R] = ((xh * cos + r * sin) * scale).astype(o_ref.dtype)


def _epi_plain(acc, o_ref):
    o_ref[0] = acc.astype(o_ref.dtype)


def _epi_silu(acc, o_ref):
    o_ref[0] = (acc * jax.nn.sigmoid(acc)).astype(o_ref.dtype)


def _head_norm(acc, g_ref, gain_ref, tn):
    g = g_ref[...]
    parts = []
    for c in range(tn // MXU_DIM):
        a = acc[:, c * MXU_DIM:(c + 1) * MXU_DIM]
        hi, lo = _split_bf16(a * a)
        ms = _dot(hi, g) + _dot(lo, g)
        parts.append(a * lax.rsqrt(ms + EPS))
    y = parts[0] if len(parts) == 1 else jnp.concatenate(parts, axis=1)
    return y * gain_ref[...]


def _epi_qs(acc, g_ref, gain_ref, o_ref, *, tn):
    o_ref[0] = (_head_norm(acc, g_ref, gain_ref, tn) * HD_S ** -0.5).astype(o_ref.dtype)


def _epi_ks1(acc, g_ref, gain_ref, o_ref, *, tn):
    o_ref[0] = _head_norm(acc, g_ref, gain_ref, tn)


def _epi_gates(acc, b_ref, o_ref):
    o_ref[0] = jax.nn.sigmoid(acc + b_ref[...]).astype(o_ref.dtype)


def _mmt_kernel(a_ref, w_ref, *rest, norm, tn, blk, side_epi, nse):
    main_extras, side_in, main_outs, side_outs, wt_ref = _split_side_refs(rest, 2 if norm else 0, 2, nse, 1)
    if norm:
        g_ref, gain_ref = main_extras
    o_ref, ob_ref = main_outs

    @pl.when((pl.program_id(1) == 0) & (pl.program_id(2) == 0))
    def _():
        wt_ref[...] = w_ref[...].T.astype(BF16)
        side_epi(lax.dot_general(side_in[0][0], wt_ref[...], _NT, preferred_element_type=F32),
                 *side_in[1:], *side_outs)

    acc = lax.dot_general(wt_ref[...], a_ref[0], _NT, preferred_element_type=F32)
    if norm:
        g = g_ref[...]
        parts = []
        for c in range(tn // MXU_DIM):
            a = acc[c * MXU_DIM:(c + 1) * MXU_DIM, :]
            hi, lo = _split_bf16(a * a)
            ms = _dot(g, hi) + _dot(g, lo)
            parts.append(a * lax.rsqrt(ms + EPS))
        acc = (parts[0] if len(parts) == 1 else jnp.concatenate(parts, axis=0)) * gain_ref[...]
    o_ref[0] = acc
    for kb in range(ob_ref.shape[1]):
        ob_ref[0, kb] = acc[:, kb * blk:(kb + 1) * blk].astype(ob_ref.dtype)


def _mm_t(a3, w, col0, n, norm_inputs, tm, tn, blk, name, side):
    bx, tx, k = a3.shape
    cb0 = col0 // tn
    norm = norm_inputs is not None
    extra, extra_specs = (), []
    if norm:
        extra = norm_inputs
        extra_specs = [pl.BlockSpec((MXU_DIM, MXU_DIM), lambda j, b, i: (0, 0)),
                       pl.BlockSpec((tn, 1), lambda j, b, i: (j, 0))]

    def jbi(spec):
        return pl.BlockSpec(spec.block_shape, lambda j, b, i, f=spec.index_map: f(b, i, j))

    s_in, s_out, s_ops, s_shapes = _side_specs(side, k, tn, jbi)
    return pl.pallas_call(
        functools.partial(_mmt_kernel, norm=norm, tn=tn, blk=blk, side_epi=side["epi"], nse=len(s_ops) - 1),
        grid=(n // tn, bx, tx // tm),
        in_specs=[pl.BlockSpec((1, tm, k), lambda j, b, i: (b, i, 0)),
                  pl.BlockSpec((k, tn), lambda j, b, i: (0, cb0 + j))] + extra_specs + s_in,
        out_specs=[pl.BlockSpec((1, tn, tm), lambda j, b, i: (b, j, i)),
                   pl.BlockSpec((1, tm // blk, tn, blk), lambda j, b, i: (b, i, j, 0))] + s_out,
        out_shape=[jax.ShapeDtypeStruct((bx, n, tx), F32),
                   jax.ShapeDtypeStruct((bx, tx // blk, n, blk), BF16)]
        + [jax.ShapeDtypeStruct((sb, sr, n), dt) for sb, sr, dt in s_shapes],
        scratch_shapes=[pltpu.VMEM((tn, k), BF16)],
        compiler_params=_cparams(("arbitrary", "arbitrary", "arbitrary")),
        name=name,
    )(a3, w, *extra, *s_ops)


def _mixer_groups(xp3, modp3, tabs_p, xs3, mods3, tabs_s, norm_gain, w_in, w_mg, b_mg, qn_gain, kn_gain, tm):
    d = xp3.shape[2]
    nd = xs3.shape[1]
    tn = PROJ_TN
    hp = _prep(xp3, modp3, norm_gain, tm)
    hs = _prep(xs3, mods3, norm_gain, nd)
    avg = jnp.where((jnp.arange(MXU_DIM)[:, None] // HD_S) == (jnp.arange(MXU_DIM)[None, :] // HD_S),
                    1.0 / HD_S, 0.0).astype(BF16)
    avg_spec = pl.BlockSpec((MXU_DIM, MXU_DIM), lambda b, i, j: (0, 0))
    gain_spec = pl.BlockSpec((1, tn), lambda b, i, j: (0, j))
    tab_spec = pl.BlockSpec((tm, DK_R), lambda b, i, j: (i, 0))
    tab_spec_s = pl.BlockSpec((nd, DK_R), lambda b, i, j: (0, 0))

    def both(w, col0, n, epi, extras, specs, dts, name, s_extras=None, s_specs=None):
        side = dict(a=hs, extras=extras if s_extras is None else s_extras,
                    specs=specs if s_specs is None else s_specs, out_dtypes=dts, epi=epi)
        outs = _mm(hp, w, col0, n, epi, tm, tn, extras, specs, dts, name, side)
        return outs[0], outs[1]

    mp, ms = {}, {}
    c0 = 0
    mp["qk_r"], ms["qk_r"] = both(w_in, c0, 2 * W_QK_R, functools.partial(_epi_rot, tn=tn), tabs_p,
                                  (tab_spec, tab_spec), (BF16,), "proj_qk_r", tabs_s, (tab_spec_s, tab_spec_s))
    c0 += 2 * W_QK_R
    mp["v_r"], ms["v_r"] = both(w_in, c0, W_V_R, _epi_plain, (), (), (BF16,), "proj_v_r")
    c0 += W_V_R
    mp["sg_r"], ms["sg_r"] = both(w_in, c0, W_V_R, _epi_silu, (), (), (BF16,), "proj_g_r")
    c0 += W_V_R
    mp["q_s"], ms["q_s"] = both(w_in, c0, W_S, functools.partial(_epi_qs, tn=tn),
                                (avg, qn_gain.reshape(1, W_S)), (avg_spec, gain_spec), (BF16,), "proj_q_s")
    c0 += W_S
    blk = min(SB_BLOCK, tm)
    side_k = dict(a=hs, extras=(avg, kn_gain.reshape(1, W_S)), specs=(avg_spec, gain_spec), out_dtypes=(F32,),
                  epi=functools.partial(_epi_ks1, tn=tn))
    mp["k_s"], mp["k_sb"], ms["k_s"] = _mm_t(hp, w_in, c0, W_S, (avg, kn_gain.reshape(W_S, 1)), tm, tn, blk,
                                            "proj_k_s_t", side_k)
    c0 += W_S
    side_v = dict(a=hs, extras=(), specs=(), out_dtypes=(F32,), epi=_epi_plain)
    mp["v_s"], mp["v_sb"], ms["v_s"] = _mm_t(hp, w_in, c0, W_S, None, tm, tn, blk, "proj_v_s_t", side_v)
    c0 += W_S
    mp["sg_s"], ms["sg_s"] = both(w_in, c0, W_S, _epi_silu, (), (), (BF16,), "proj_g_s")
    mp["gates"], ms["gates"] = both(w_mg, 0, 2 * d, _epi_gates, (b_mg.reshape(1, 2 * d),), (gain_spec,), (BF16,),
                                    "proj_gates")
    return mp, ms


def _ret_kernel(q_ref, k_ref, v_ref, sg_ref, gn_ref, s0_ref, o_ref, st_ref, *, rows, chunk):
    @pl.when(pl.program_id(1) == 0)
    def _():
        st_ref[...] = s0_ref[...]

    _ret_heads(q_ref, k_ref, v_ref, sg_ref, gn_ref, o_ref, st_ref, range(H_R), rows, chunk)


def _ret_heads(q_ref, k_ref, v_ref, sg_ref, gn_ref, o_ref, st_ref, heads, rows, chunk):
    ri = lax.broadcasted_iota(jnp.int32, (rows, rows), 0)
    ci = lax.broadcasted_iota(jnp.int32, (rows, rows), 1)
    diff = (ri - ci).astype(F32)
    idx = lax.broadcasted_iota(jnp.int32, (rows, 1), 0).astype(F32)
    for h in heads:
        lg = math.log(1.0 - 2.0 ** (-5.0 - h))
        decay = jnp.where(diff >= 0.0, jnp.exp(lg * jnp.maximum(diff, 0.0)), 0.0)
        q_dec = jnp.exp(lg * (idx + 1.0))
        k_dec = jnp.exp(lg * (chunk - 1.0 - idx))
        sl = slice(h * DV_R, (h + 1) * DV_R)
        for s in range(q_ref.shape[0]):
            q = q_ref[s, :, h * DK_R:(h + 1) * DK_R]
            k = k_ref[s, :, h * DK_R:(h + 1) * DK_R]
            v = v_ref[s, :, sl]
            st = st_ref[s, h]
            scores = lax.dot_general(q, k, _NT, preferred_element_type=F32) * decay
            inner = _dot(scores.astype(BF16), v)
            cross = _dot(q, st.astype(BF16)) * q_dec
            kd = (k.astype(F32) * k_dec).astype(BF16)
            st_ref[s, h] = math.exp(lg * chunk) * st + lax.dot_general(kd, v, _TN, preferred_element_type=F32)
            o = inner + cross
            inv = lax.rsqrt(jnp.mean(o * o, axis=-1, keepdims=True) + EPS)
            o_ref[s, :, sl] = ((o * inv) * gn_ref[:, sl] * sg_ref[s, :, sl].astype(F32)).astype(o_ref.dtype)


def _retention(qk_r, v_r, sg_r, gn_r, state0, rows, chunk, bb):
    bx, tx, _ = v_r.shape
    nc = tx // rows
    return pl.pallas_call(
        functools.partial(_ret_kernel, rows=rows, chunk=chunk),
        grid=(bx // bb, nc),
        in_specs=[pl.BlockSpec((bb, rows, W_QK_R), lambda b, c: (b, c, 0)),
                  pl.BlockSpec((bb, rows, W_QK_R), lambda b, c: (b, c, 1)),
                  pl.BlockSpec((bb, rows, W_V_R), lambda b, c: (b, c, 0)),
                  pl.BlockSpec((bb, rows, W_V_R), lambda b, c: (b, c, 0)),
                  pl.BlockSpec((1, W_V_R), lambda b, c: (0, 0)),
                  pl.BlockSpec((bb, H_R, DK_R, DV_R), lambda b, c: (b, 0, 0, 0))],
        out_specs=[pl.BlockSpec((bb, rows, W_V_R), lambda b, c: (b, c, 0)),
                   pl.BlockSpec((bb, H_R, DK_R, DV_R), lambda b, c: (b, 0, 0, 0))],
        out_shape=[jax.ShapeDtypeStruct((bx, tx, W_V_R), BF16),
                   jax.ShapeDtypeStruct((bx, H_R, DK_R, DV_R), F32)],
        compiler_params=_cparams(("parallel", "arbitrary")),
        name="retention",
    )(qk_r, qk_r, v_r, sg_r, gn_r.reshape(1, W_V_R), state0)


def _sb_gate(z, tri, mask):
    sp = jnp.log(1.0 + jnp.exp2(jnp.abs(z) * -LOG2E))
    log_beta = jnp.minimum(z, 0.0) - sp
    log_1mb = log_beta - z
    if mask is not None:
        log_1mb = jnp.where(mask, log_1mb, 0.0)
    c = tri.shape[0]
    parts, sums = [], []
    for ci in range(z.shape[1] // c):
        part = log_1mb[:, ci * c:(ci + 1) * c]
        parts.append(part.astype(BF16))
        sums.append(jnp.sum(part, axis=1, keepdims=True))
    return log_beta, parts, sums


def _sb_suffix(parts, tri):
    return [_dot(p, tri) for p in parts]


def _sb_weights(log_beta, local, sums, carry, mask):
    afters = [None] * len(local)
    for ci in reversed(range(len(local))):
        afters[ci] = local[ci] + carry
        carry = carry + sums[ci]
    after = afters[0] if len(afters) == 1 else jnp.concatenate(afters, axis=1)
    a = jnp.exp(log_beta + after)
    if mask is not None:
        a = jnp.where(mask, a, 0.0)
    return a.astype(BF16), carry


def _sb_values(a, vb):
    return lax.dot_general(a, vb, _NT, preferred_element_type=F32)


def _sb_step(qh, kb, vb, bias, tri, carry, mask):
    log_beta, parts, sums = _sb_gate(_dot(qh, kb) + bias, tri, mask)
    a, carry = _sb_weights(log_beta, _sb_suffix(parts, tri), sums, carry, mask)
    return carry, _sb_values(a, vb)


def _tri(n):
    return (lax.broadcasted_iota(jnp.int32, (n, n), 0) > lax.broadcasted_iota(jnp.int32, (n, n), 1)).astype(BF16)


def _sbp_kernel(bias_ref, q_ref, k_ref, v_ref, sg_ref, o_ref, qh_ref, carry_ref, acc_ref,
                z_ref, lb_ref, hl_ref, sum_ref, loc_ref, a_ref, *, blk, nh):
    hg = pl.program_id(1)
    i = pl.program_id(2)
    w2 = 2 * HD_S
    lane = lax.broadcasted_iota(jnp.int32, (1, w2), 1)
    tri = _tri(blk)
    diag = (lax.broadcasted_iota(jnp.int32, (blk, blk), 1) < lax.broadcasted_iota(jnp.int32, (blk, blk), 0))
    for h in range(nh):
        q2 = q_ref[0, :, (h // 2) * w2:(h // 2 + 1) * w2]
        qh_ref[h, :, :w2] = jnp.where((lane // HD_S) == h % 2, q2, jnp.zeros_like(q2))
        b = jnp.full((blk, w2), bias_ref[hg * nh + h], F32)
        b0 = b.astype(BF16).astype(F32)
        b1 = (b - b0).astype(BF16).astype(F32)
        b2 = b - b0 - b1
        cols = jnp.where(lane == 0, b0, jnp.where(lane == 1, b1, jnp.where(lane == 2, b2, 0.0)))
        qh_ref[h, :, w2:] = cols.astype(BF16)
    ones = jnp.ones((w2, blk), BF16)
    carry_ref[...] = jnp.zeros_like(carry_ref)
    acc_ref[...] = jnp.zeros_like(acc_ref)

    def rows(h):
        return slice((h // 2) * w2, (h // 2 + 1) * w2)

    def logits_of(h, j):
        return _dot(qh_ref[h], jnp.concatenate([k_ref[0, j, rows(h), :], ones], axis=0))

    def fold(j, mask):
        logits, gates, local, weights = {}, {}, {}, {}

        for n in range(nh + 4):
            if n < nh:
                logits[n] = logits_of(n, j)
            h = n - 2
            if 0 <= h < nh:
                local[h] = _sb_suffix(gates[h][1], tri)
            h = n - 4
            if 0 <= h < nh:
                acc_ref[h] += _sb_values(weights.pop(h), v_ref[0, j, rows(h), :])
            h = n - 1
            if 0 <= h < nh:
                gates[h] = _sb_gate(logits.pop(h), tri, mask)
            h = n - 3
            if 0 <= h < nh:
                log_beta, _, sums = gates.pop(h)
                weights[h], carry_ref[h] = _sb_weights(log_beta, local.pop(h), sums, carry_ref[h], mask)

    fold(i, diag)

    lags = dict(gate=1, suffix=2, weights=3, values=4)

    def trip(j_new, j_old, fresh, wrapped):
        def live(s, lag):
            return fresh if s >= lag else wrapped

        held = {}

        def put(ref, h, lag_p, lag_c, val):
            if (h + lag_p >= nh) == (h + lag_c >= nh):
                held[(id(ref), h)] = val
            else:
                ref[h] = val

        def get(ref, h, lag_p, lag_c):
            if (h + lag_p >= nh) == (h + lag_c >= nh):
                return held.pop((id(ref), h))
            return ref[h]

        lg, ls, lw, lv = lags["gate"], lags["suffix"], lags["weights"], lags["values"]
        for s in range(nh):
            if fresh:
                put(z_ref, s, 0, lg, logits_of(s, j_new))
            if live(s, ls):
                h = (s - ls) % nh
                put(loc_ref, h, ls, lw, _sb_suffix([get(hl_ref, h, lg, ls)], tri)[0])
            if live(s, lv):
                h = (s - lv) % nh
                jv = j_new if s >= lv else j_old
                acc_ref[h] += _sb_values(get(a_ref, h, lw, lv), v_ref[0, jv, rows(h), :])
            if live(s, lg):
                h = (s - lg) % nh
                log_beta, parts, sums = _sb_gate(get(z_ref, h, 0, lg), tri, None)
                put(lb_ref, h, lg, lw, log_beta)
                put(hl_ref, h, lg, ls, parts[0])
                put(sum_ref, h, lg, lw, sums[0])
            if live(s, lw):
                h = (s - lw) % nh
                a, carry_ref[h] = _sb_weights(get(lb_ref, h, lg, lw), [get(loc_ref, h, ls, lw)],
                                              [get(sum_ref, h, lg, lw)], carry_ref[h], None)
                put(a_ref, h, lw, lv, a)

    @pl.when(i >= 1)
    def _():
        trip(i - 1, None, True, False)

        def body(t, c):
            trip(i - 1 - t, i - t, True, True)
            return c

        lax.fori_loop(1, i, body, 0)
        trip(None, 0, False, True)

    for p in range(nh // 2):
        o = jnp.where(lane < HD_S, acc_ref[2 * p], acc_ref[2 * p + 1])
        sl = slice(p * w2, (p + 1) * w2)
        o_ref[0, :, sl] = (o * sg_ref[0, :, sl].astype(F32)).astype(o_ref.dtype)


def _sb_prompt(q_s, k_sb, v_sb, sg_s, sb_bias):
    bx, s, _ = q_s.shape
    nkb, blk = k_sb.shape[1], k_sb.shape[3]
    nh = SB_HEADS_PER_STEP
    w2 = nh * HD_S
    return pl.pallas_call(
        functools.partial(_sbp_kernel, blk=blk, nh=nh),
        grid=(bx, H_S // nh, s // blk),
        in_specs=[pl.BlockSpec(memory_space=pltpu.SMEM),
                  pl.BlockSpec((1, blk, w2), lambda b, hp, i: (b, i, hp)),
                  pl.BlockSpec((1, nkb, w2, blk), lambda b, hp, i: (b, 0, hp, 0)),
                  pl.BlockSpec((1, nkb, w2, blk), lambda b, hp, i: (b, 0, hp, 0)),
                  pl.BlockSpec((1, blk, w2), lambda b, hp, i: (b, i, hp))],
        out_specs=pl.BlockSpec((1, blk, w2), lambda b, hp, i: (b, i, hp)),
        out_shape=jax.ShapeDtypeStruct((bx, s, W_S), BF16),
        scratch_shapes=[pltpu.VMEM((nh, blk, 4 * HD_S), BF16), pltpu.VMEM((nh, blk, 1), F32),
                        pltpu.VMEM((nh, blk, 2 * HD_S), F32),
                        pltpu.VMEM((nh, blk, blk), F32), pltpu.VMEM((nh, blk, blk), F32),
                        pltpu.VMEM((nh, blk, blk), BF16), pltpu.VMEM((nh, blk, 1), F32),
                        pltpu.VMEM((nh, blk, blk), F32), pltpu.VMEM((nh, blk, blk), BF16)],
        compiler_params=_cparams(("parallel", "parallel", "arbitrary")),
        name="sb_prompt",
    )(sb_bias, q_s, k_sb, v_sb, sg_s)


def _sbs_kernel(pt_ref, bias_ref, q_ref, kn_ref, vn_ref, sg_ref, *rest, npg, page, td, ret_chunk):
    k_refs = rest[:npg]
    v_refs = rest[npg:2 * npg]
    rest = rest[2 * npg:]
    g = pl.program_id(1)
    if ret_chunk:
        rq_ref, rk_ref, rv_ref, rsg_ref, gn_ref, o_ref, rbr_ref, rstate_ref, qbd_ref, carry_ref, acc_ref, rst_ref = rest
        first_seq = pl.program_id(0) == 0
        last_seq = pl.program_id(0) == pl.num_programs(0) - 1

        @pl.when(first_seq & (g == 0))
        def _():
            rst_ref[...] = jnp.zeros_like(rst_ref)

        for h in range(H_R):
            @pl.when(g == h)
            def _(h=h):
                _ret_heads(rq_ref, rk_ref, rv_ref, rsg_ref, gn_ref, rbr_ref, rst_ref, [h], ret_chunk, ret_chunk)

        @pl.when(last_seq & (g == H_R - 1))
        def _():
            rstate_ref[...] = rst_ref[...]
    else:
        o_ref, qbd_ref, carry_ref, acc_ref = rest
    rows = H_S * td
    tri = _tri(MXU_DIM if (npg * page) % MXU_DIM == 0 else page)
    bias = bias_ref[...]

    @pl.when(g == 0)
    def _():
        qt = jnp.concatenate([q_ref[0]] * H_S, axis=0)
        rh = lax.broadcasted_iota(jnp.int32, (rows, W_S), 0) // td
        lh = lax.broadcasted_iota(jnp.int32, (rows, W_S), 1) // HD_S
        qbd = jnp.where(rh == lh, qt, 0.0).astype(BF16)
        qbd_ref[...] = qbd
        pad = jnp.zeros((page - td, W_S), F32)
        kn = jnp.concatenate([kn_ref[0], pad], axis=0).astype(BF16)
        vn = jnp.concatenate([vn_ref[0], pad], axis=0).astype(BF16)
        t_of_row = lax.broadcasted_iota(jnp.int32, (rows, page), 0) % td
        mask = lax.broadcasted_iota(jnp.int32, (rows, page), 1) < t_of_row
        z = lax.dot_general(qbd, kn, _NT, preferred_element_type=F32) + bias
        log_beta, parts, sums = _sb_gate(z, _tri(page), mask)
        a, carry = _sb_weights(log_beta, _sb_suffix(parts, _tri(page)), sums, jnp.zeros((rows, 1), F32), mask)
        carry_ref[...] = carry
        acc_ref[...] = _dot(a, vn)

    kb = jnp.concatenate([k_refs[p][0].astype(BF16) for p in reversed(range(npg))], axis=1)
    vb = jnp.concatenate([v_refs[p][0].astype(BF16) for p in reversed(range(npg))], axis=1)
    carry, contrib = _sb_step(qbd_ref[...], kb, vb, bias, tri, carry_ref[...], None)
    acc_ref[...] += contrib
    carry_ref[...] = carry

    @pl.when(g == pl.num_programs(1) - 1)
    def _():
        acc = acc_ref[...]
        lh = lax.broadcasted_iota(jnp.int32, (td, W_S), 1) // HD_S
        out = jnp.zeros((td, W_S), F32)
        for h in range(H_S):
            out = jnp.where(lh == h, acc[h * td:(h + 1) * td, :], out)
        o_ref[0] = out * sg_ref[0]


def _ret_can_ride(v_r, bd, n_pages):
    npg = min(PAGES_PER_STEP, n_pages)
    return v_r.shape[1] // CHUNK == bd and n_pages // npg == H_R


def _sb_sample(q_s, k_new, v_new, sg_s, cache_k, cache_v, page_table, sb_bias, ret=None):
    bd, td, _ = q_s.shape
    n_pages = page_table.shape[1]
    page = cache_k.shape[2]
    npg = min(PAGES_PER_STEP, n_pages)
    rows = H_S * td
    bias_col = jnp.repeat(sb_bias, td).reshape(rows, 1)

    def page_spec(p):
        return pl.BlockSpec((1, W_S, page), lambda b, g, pt: (pt[b, n_pages - 1 - (g * npg + p)], 0, 0))

    tok_spec = pl.BlockSpec((1, td, W_S), lambda b, g, pt: (b, 0, 0))
    in_specs = ([pl.BlockSpec((rows, 1), lambda b, g, pt: (0, 0)), tok_spec, tok_spec, tok_spec, tok_spec]
                + [page_spec(p) for p in range(npg)] * 2)
    out_specs = [tok_spec]
    out_shape = [jax.ShapeDtypeStruct((bd, td, W_S), F32)]
    scratch = [pltpu.VMEM((rows, W_S), BF16), pltpu.VMEM((rows, 1), F32), pltpu.VMEM((rows, W_S), F32)]
    operands = [page_table, bias_col, q_s, k_new, v_new, sg_s, *([cache_k] * npg), *([cache_v] * npg)]
    if ret is not None:
        qk_r, v_r, sg_r, gn_r = ret
        bp, s, _ = v_r.shape
        wide = pl.BlockSpec((bp, CHUNK, W_V_R), lambda b, g, pt: (0, b, 0))
        in_specs += [pl.BlockSpec((bp, CHUNK, W_QK_R), lambda b, g, pt: (0, b, 0)),
                     pl.BlockSpec((bp, CHUNK, W_QK_R), lambda b, g, pt: (0, b, 1)), wide, wide,
                     pl.BlockSpec((1, W_V_R), lambda b, g, pt: (0, 0))]
        state_spec = pl.BlockSpec((bp, H_R, DK_R, DV_R), lambda b, g, pt: (0, 0, 0, 0))
        out_specs += [wide, state_spec]
        out_shape += [jax.ShapeDtypeStruct((bp, s, W_V_R), BF16), jax.ShapeDtypeStruct((bp, H_R, DK_R, DV_R), F32)]
        scratch += [pltpu.VMEM((bp, H_R, DK_R, DV_R), F32)]
        operands += [qk_r, qk_r, v_r, sg_r, gn_r.reshape(1, W_V_R)]
    grid_spec = pltpu.PrefetchScalarGridSpec(
        num_scalar_prefetch=1,
        grid=(bd, n_pages // npg),
        in_specs=in_specs,
        out_specs=out_specs,
        scratch_shapes=scratch,
    )
    return pl.pallas_call(
        functools.partial(_sbs_kernel, npg=npg, page=page, td=td, ret_chunk=CHUNK if ret is not None else 0),
        grid_spec=grid_spec,
        out_shape=out_shape,
        compiler_params=_cparams(("arbitrary", "arbitrary"), VMEM_LIMIT_PAGED),
        name="sb_sample",
    )(*operands)


def _merge_kernel(ar_ref, wr_ref, as_ref, ws_ref, gr_ref, gs_ref, o_ref):
    y_r = _dot(ar_ref[0], wr_ref[...])
    y_s = _dot(as_ref[0], ws_ref[...])
    o_ref[0] = (gr_ref[0].astype(F32) * y_r + gs_ref[0].astype(F32) * y_s).astype(o_ref.dtype)


def _merge(br_r, br_s, gates, w_pr_b, w_ps_b, tm):
    bx, tx, _ = br_r.shape
    d = w_pr_b.shape[1]
    tn = PROJ_TN
    nb = d // tn
    return pl.pallas_call(
        _merge_kernel,
        grid=(bx, tx // tm, nb),
        in_specs=[pl.BlockSpec((1, tm, W_V_R), lambda b, i, j: (b, i, 0)),
                  pl.BlockSpec((W_V_R, tn), lambda b, i, j: (0, j)),
                  pl.BlockSpec((1, tm, W_S), lambda b, i, j: (b, i, 0)),
                  pl.BlockSpec((W_S, tn), lambda b, i, j: (0, j)),
                  pl.BlockSpec((1, tm, tn), lambda b, i, j: (b, i, j)),
                  pl.BlockSpec((1, tm, tn), lambda b, i, j: (b, i, nb + j))],
        out_specs=pl.BlockSpec((1, tm, tn), lambda b, i, j: (b, i, j)),
        out_shape=jax.ShapeDtypeStruct((bx, tx, d), BF16),
        compiler_params=_cparams(("parallel", "parallel", "arbitrary")),
        name="merge",
    )(br_r, w_pr_b, br_s, w_ps_b, gates, gates)


def _out_kernel(m_ref, w_ref, x_ref, g_ref, o_ref):
    o_ref[0] = x_ref[0] + g_ref[0] * _dot(m_ref[0], w_ref[...])


def _out_proj(merged, w_out_b, x3, mod3, tm):
    bx, tx, d = x3.shape
    tn = PROJ_TN
    return pl.pallas_call(
        _out_kernel,
        grid=(bx, tx // tm, d // tn),
        in_specs=[pl.BlockSpec((1, tm, d), lambda b, i, j: (b, i, 0)),
                  pl.BlockSpec((d, tn), lambda b, i, j: (0, j)),
                  pl.BlockSpec((1, tm, tn), lambda b, i, j: (b, i, j)),
                  _mod_spec(mod3, 2, tm, tn, d, 3)],
        out_specs=pl.BlockSpec((1, tm, tn), lambda b, i, j: (b, i, j)),
        out_shape=jax.ShapeDtypeStruct((bx, tx, d), F32),
        compiler_params=_cparams(("parallel", "parallel", "arbitrary")),
        name="out_proj",
    )(merged, w_out_b, x3, mod3)


def _rope_tables(pos):
    half = DK_R // 2
    inv_freq = ROPE_BASE ** (-jnp.arange(half, dtype=F32) / half)
    ang = pos.astype(F32)[:, None] * inv_freq[None, :]
    cos = jnp.cos(ang)
    sin = jnp.sin(ang)
    return jnp.concatenate([cos, cos], axis=1), jnp.concatenate([-sin, sin], axis=1)


def kernel(x_prompt, x_sample, cache_sb_k, cache_sb_v, state_ret, page_table, c_prompt, c_sample,
           norm_gain, w_ada, b_ada, w_in, qn_gain, kn_gain, sb_bias, gn_r, w_pr, w_ps, w_mg, b_mg, w_out):
    b, s, d = x_prompt.shape
    bd, td, _ = x_sample.shape
    n_pages = page_table.shape[1]
    page = cache_sb_k.shape[1]
    past = n_pages * page
    nd = bd * td
    assert s % SB_BLOCK == 0 or s < SB_BLOCK
    assert s % CHUNK == 0 and td <= 16 and page == LANES

    w_pr_b, w_ps_b, w_out_b = w_pr.astype(BF16), w_ps.astype(BF16), w_out.astype(BF16)

    mod = _ada(jnp.concatenate([c_prompt, c_sample], axis=0), w_ada, b_ada)
    mod_p = mod[:b].reshape(b, 1, 3 * d)
    mod_s = jnp.repeat(mod[b:], td, axis=0).reshape(1, nd, 3 * d)

    tm_p = min(1024, s)
    cos_p, sin_p = _rope_tables(jnp.arange(s))
    cos_s, sin_s = _rope_tables(jnp.tile(past + jnp.arange(td), bd))

    xs3 = x_sample.reshape(1, nd, d)
    mp, ms = _mixer_groups(x_prompt, mod_p, (cos_p, sin_p), xs3, mod_s, (cos_s, sin_s), norm_gain, w_in, w_mg,
                           b_mg, qn_gain, kn_gain, tm_p)

    rows = 16

    def pad_tok(a):
        a = a.reshape(bd, td, a.shape[-1])
        return jnp.pad(a, ((0, 0), (0, rows - td), (0, 0)))

    br_r2, ret_state_sample = _retention(pad_tok(ms["qk_r"]), pad_tok(ms["v_r"]), pad_tok(ms["sg_r"]), gn_r,
                                         state_ret, rows, td, math.gcd(bd, 4))
    br_r2 = br_r2[:, :td].reshape(1, nd, W_V_R)

    def tok_f32(a):
        return a.astype(F32).reshape(bd, td, W_S)

    def pages(c):
        return c.transpose(0, 2, 3, 1).reshape(-1, W_S, page)

    ride = _ret_can_ride(mp["v_r"], bd, n_pages)
    outs = _sb_sample(tok_f32(ms["q_s"]), tok_f32(ms["k_s"]), tok_f32(ms["v_s"]), tok_f32(ms["sg_s"]),
                      pages(cache_sb_k), pages(cache_sb_v), page_table, sb_bias,
                      (mp["qk_r"], mp["v_r"], mp["sg_r"], gn_r) if ride else None)
    br_s2 = outs[0].astype(BF16).reshape(1, nd, W_S)
    merged2 = _merge(br_r2, br_s2, ms["gates"], w_pr_b, w_ps_b, nd)
    y_s = _out_proj(merged2, w_out_b, xs3, mod_s, nd)

    if ride:
        br_r, ret_state_prompt = outs[1], outs[2]
    else:
        br_r, ret_state_prompt = _retention(mp["qk_r"], mp["v_r"], mp["sg_r"], gn_r,
                                            jnp.zeros((b, H_R, DK_R, DV_R), F32), CHUNK, CHUNK, b)
    br_s = _sb_prompt(mp["q_s"], mp["k_sb"], mp["v_sb"], mp["sg_s"], sb_bias)
    merged = _merge(br_r, br_s, mp["gates"], w_pr_b, w_ps_b, tm_p)
    y_p = _out_proj(merged, w_out_b, x_prompt, mod_p, tm_p)

    def rows_layout(a):
        return a.reshape(b, H_S, HD_S, s).transpose(0, 3, 1, 2)

    return (y_p, y_s.reshape(bd, td, d), ret_state_prompt, ret_state_sample,
            rows_layout(mp["k_s"]), rows_layout(mp["v_s"]),
            ms["k_s"].reshape(bd, td, H_S, HD_S), ms["v_s"].reshape(bd, td, H_S, HD_S))
```
